```python
import math
import jax, jax.numpy as jnp
from jax import lax
import numpy as np

D_MODEL = 1024
BATCH = 8
SEQ = 8192
DEPTH = 4

N_MIXERS = 2
N_SB = (DEPTH + 1) // 2
N_GLA = DEPTH // 2

ALPHA = (2.0 * DEPTH) ** 0.25
BETA = (8.0 * DEPTH) ** -0.25

SB_HEADS = 16
SB_HEAD_DIM = D_MODEL // SB_HEADS
SB_BLOCK = 128

GLA_HEADS = 4
GLA_KEY_DIM = D_MODEL // 2
GLA_VALUE_DIM = D_MODEL
GLA_DK = GLA_KEY_DIM // GLA_HEADS
GLA_DV = GLA_VALUE_DIM // GLA_HEADS
GLA_GATE_RANK = 16
GLA_GATE_TAU = 16.0
GLA_CHUNK = 64
GLA_IN_DIM = 2 * GLA_KEY_DIM + 2 * GLA_VALUE_DIM

D_FF = ((8 * D_MODEL // 3 + 127) // 128) * 128
CONV_W = 3

LN_EPS = 1e-5

kernel_name = "hybrid_stickbreak_gla_convffn_deepnorm"


def layer_norm(x, g, b):
    xf = x.astype(jnp.float32)
    mu = jnp.mean(xf, axis=-1, keepdims=True)
    var = jnp.mean(jnp.square(xf - mu), axis=-1, keepdims=True)
    y = (xf - mu) * lax.rsqrt(var + LN_EPS)
    return (y * g.astype(jnp.float32) + b.astype(jnp.float32)).astype(x.dtype)


def stick_breaking_attention(x, w_qkv, w_o):
    B, S, _ = x.shape
    qkv = (x @ w_qkv).reshape(B, S, 3, SB_HEADS, SB_HEAD_DIM)
    q, k, v = [jnp.transpose(qkv[:, :, i], (0, 2, 1, 3)) for i in range(3)]
    scale = SB_HEAD_DIM ** -0.5
    outs = []
    for blk in range(S // SB_BLOCK):
        q0 = blk * SB_BLOCK
        kv_len = q0 + SB_BLOCK
        q_b = q[:, :, q0:kv_len]
        k_b = k[:, :, :kv_len]
        v_b = v[:, :, :kv_len]
        z = jnp.einsum('bhtd,bhsd->bhts', q_b, k_b).astype(jnp.float32) * scale
        t_idx = q0 + jnp.arange(SB_BLOCK)[:, None]
        s_idx = jnp.arange(kv_len)[None, :]
        mask = s_idx < t_idx
        log_1m_beta = jnp.where(mask, jax.nn.log_sigmoid(-z), 0.0)
        tail = lax.cumsum(log_1m_beta, axis=3, reverse=True) - log_1m_beta
        log_a = jax.nn.log_sigmoid(z) + tail
        a = jnp.where(mask, jnp.exp(log_a), 0.0)
        outs.append(jnp.einsum('bhts,bhsd->bhtd', a, v_b.astype(jnp.float32)))
    o = jnp.concatenate(outs, axis=2)
    o = jnp.transpose(o, (0, 2, 1, 3)).reshape(B, S, SB_HEADS * SB_HEAD_DIM).astype(x.dtype)
    return o @ w_o


def gated_linear_attention(x, w_in, w_a1, w_a2, b_a, norm_g, w_o):
    B, S, _ = x.shape
    N, C = S // GLA_CHUNK, GLA_CHUNK
    proj = x @ w_in
    q = proj[..., :GLA_KEY_DIM]
    k = proj[..., GLA_KEY_DIM:2 * GLA_KEY_DIM]
    v = proj[..., 2 * GLA_KEY_DIM:2 * GLA_KEY_DIM + GLA_VALUE_DIM]
    r = proj[..., 2 * GLA_KEY_DIM + GLA_VALUE_DIM:]
    g = jax.nn.log_sigmoid(((x @ w_a1) @ w_a2 + b_a).astype(jnp.float32)) / GLA_GATE_TAU

    def to_chunks(t, d):
        return jnp.transpose(t.reshape(B, N, C, GLA_HEADS, d), (0, 3, 1, 2, 4))

    q = to_chunks(q.astype(jnp.float32), GLA_DK) * (GLA_DK ** -0.5)
    k = to_chunks(k.astype(jnp.float32), GLA_DK)
    v = to_chunks(v.astype(jnp.float32), GLA_DV)
    g = to_chunks(g, GLA_DK)
    b = jnp.cumsum(g, axis=3)
    b_last = b[:, :, :, -1:, :]

    q_dec = q * jnp.exp(b)
    k_inv = k * jnp.exp(-b)
    attn = jnp.einsum('bhnti,bhnsi->bhnts', q_dec, k_inv)
    causal = jnp.tril(jnp.ones((C, C), dtype=bool))
    attn = jnp.where(causal, attn, 0.0)
    o_intra = jnp.einsum('bhnts,bhnsv->bhntv', attn, v)

    k_dec = k * jnp.exp(b_last - b)
    chunk_kv = jnp.einsum('bhnsi,bhnsv->bhniv', k_dec, v)
    chunk_decay = jnp.exp(b_last[:, :, :, 0, :])

    def step(state, inp):
        decay, kv = inp
        return decay[..., None] * state + kv, state

    init = jnp.zeros((B, GLA_HEADS, GLA_DK, GLA_DV), jnp.float32)
    _, states = lax.scan(step, init, (jnp.moveaxis(chunk_decay, 2, 0), jnp.moveaxis(chunk_kv, 2, 0)))
    o_inter = jnp.einsum('bhnti,nbhiv->bhntv', q_dec, states)

    o = o_intra + o_inter
    o = jnp.transpose(o, (0, 2, 3, 1, 4)).reshape(B, S, GLA_HEADS, GLA_DV)
    o = o * lax.rsqrt(jnp.mean(jnp.square(o), axis=-1, keepdims=True) + LN_EPS) * norm_g.astype(jnp.float32)
    o = o * jax.nn.silu(r.astype(jnp.float32).reshape(B, S, GLA_HEADS, GLA_DV))
    return o.reshape(B, S, GLA_VALUE_DIM).astype(x.dtype) @ w_o


def conv_ffn(x, w_up, conv_w, conv_b, w_down):
    h = x @ w_up
    h = lax.conv_general_dilated(
        h, conv_w.astype(h.dtype)[:, None, :], window_strides=(1,),
        padding=[(CONV_W - 1, 0)], dimension_numbers=('NWC', 'WIO', 'NWC'),
        feature_group_count=2 * D_FF) + conv_b
    u, gte = h[..., :D_FF], h[..., D_FF:]
    return (jax.nn.silu(gte) * u) @ w_down


def _fwd_setup_inputs(seed: int = 0) -> dict:
    key = jax.random.key(seed)
    ks = jax.random.split(key, 20)
    nrm = jax.random.normal
    f32 = jnp.float32
    x = nrm(ks[0], (BATCH, SEQ, D_MODEL), f32)

    sb_col = jnp.concatenate([jnp.ones((2 * D_MODEL,), f32), jnp.full((D_MODEL,), BETA, f32)])
    sb_w_qkv = nrm(ks[1], (N_SB, D_MODEL, 3 * D_MODEL), f32) * D_MODEL ** -0.5 * sb_col
    sb_w_o = nrm(ks[2], (N_SB, D_MODEL, D_MODEL), f32) * D_MODEL ** -0.5 * BETA

    gla_col = jnp.concatenate([jnp.ones((2 * GLA_KEY_DIM,), f32), jnp.full((GLA_VALUE_DIM,), BETA, f32),
                               jnp.ones((GLA_VALUE_DIM,), f32)])
    gla_w_in = nrm(ks[3], (N_GLA, D_MODEL, GLA_IN_DIM), f32) * D_MODEL ** -0.5 * gla_col
    gla_w_a1 = nrm(ks[4], (N_GLA, D_MODEL, GLA_GATE_RANK), f32) * D_MODEL ** -0.5
    gla_w_a2 = nrm(ks[5], (N_GLA, GLA_GATE_RANK, GLA_KEY_DIM), f32) * GLA_GATE_RANK ** -0.5
    gla_b_a = nrm(ks[6], (N_GLA, GLA_KEY_DIM), f32) * 0.1 + 1.0
    gla_norm_g = 1.0 + 0.01 * nrm(ks[7], (N_GLA, GLA_DV), f32)
    gla_w_o = nrm(ks[8], (N_GLA, GLA_VALUE_DIM, D_MODEL), f32) * GLA_VALUE_DIM ** -0.5 * BETA

    ffn_w_up = nrm(ks[9], (DEPTH, D_MODEL, 2 * D_FF), f32) * D_MODEL ** -0.5 * BETA
    ffn_conv_w = nrm(ks[10], (DEPTH, CONV_W, 2 * D_FF), f32) * CONV_W ** -0.5
    ffn_conv_b = nrm(ks[11], (DEPTH, 2 * D_FF), f32) * 0.01
    ffn_w_down = nrm(ks[12], (DEPTH, D_FF, D_MODEL), f32) * D_FF ** -0.5 * BETA

    ln_mix_g = 1.0 + 0.01 * nrm(ks[13], (DEPTH, D_MODEL), f32)
    ln_mix_b = 0.01 * nrm(ks[14], (DEPTH, D_MODEL), f32)
    ln_ffn_g = 1.0 + 0.01 * nrm(ks[15], (DEPTH, D_MODEL), f32)
    ln_ffn_b = 0.01 * nrm(ks[16], (DEPTH, D_MODEL), f32)
    return {"x": x, "sb_w_qkv": sb_w_qkv, "sb_w_o": sb_w_o,
            "gla_w_in": gla_w_in, "gla_w_a1": gla_w_a1, "gla_w_a2": gla_w_a2, "gla_b_a": gla_b_a,
            "gla_norm_g": gla_norm_g, "gla_w_o": gla_w_o,
            "ffn_w_up": ffn_w_up, "ffn_conv_w": ffn_conv_w, "ffn_conv_b": ffn_conv_b, "ffn_w_down": ffn_w_down,
            "ln_mix_g": ln_mix_g, "ln_mix_b": ln_mix_b, "ln_ffn_g": ln_ffn_g, "ln_ffn_b": ln_ffn_b}


def _fwd_reference(x, sb_w_qkv, sb_w_o, gla_w_in, gla_w_a1, gla_w_a2, gla_b_a, gla_norm_g, gla_w_o,
              ffn_w_up, ffn_conv_w, ffn_conv_b, ffn_w_down, ln_mix_g, ln_mix_b, ln_ffn_g, ln_ffn_b):
    for i in range(DEPTH):
        j = i // N_MIXERS
        if i % N_MIXERS == 0:
            y = stick_breaking_attention(x, sb_w_qkv[j], sb_w_o[j])
        else:
            y = gated_linear_attention(x, gla_w_in[j], gla_w_a1[j], gla_w_a2[j], gla_b_a[j],
                                       gla_norm_g[j], gla_w_o[j])
        x = layer_norm(ALPHA * x + y, ln_mix_g[i], ln_mix_b[i])
        y = conv_ffn(x, ffn_w_up[i], ffn_conv_w[i], ffn_conv_b[i], ffn_w_down[i])
        x = layer_norm(ALPHA * x + y, ln_ffn_g[i], ln_ffn_b[i])
    return x


import jax as _jax
import jax.numpy as _jnp

TWIN_FORMAT = 'train_step'
FWD_PARAMS = ['x', 'sb_w_qkv', 'sb_w_o', 'gla_w_in', 'gla_w_a1', 'gla_w_a2', 'gla_b_a', 'gla_norm_g', 'gla_w_o', 'ffn_w_up', 'ffn_conv_w', 'ffn_conv_b', 'ffn_w_down', 'ln_mix_g', 'ln_mix_b', 'ln_ffn_g', 'ln_ffn_b']
TWIN_WEIGHTS = ['sb_w_qkv', 'sb_w_o', 'gla_w_in', 'gla_w_a1', 'gla_w_a2', 'gla_b_a', 'gla_norm_g', 'gla_w_o', 'ffn_w_up', 'ffn_conv_w', 'ffn_conv_b', 'ffn_w_down', 'ln_mix_g', 'ln_mix_b', 'ln_ffn_g', 'ln_ffn_b']
TWIN_DIFF_INPUT = 'x'
TWIN_INPUTS = ['x', 'sb_w_qkv', 'sb_w_o', 'gla_w_in', 'gla_w_a1', 'gla_w_a2', 'gla_b_a', 'gla_norm_g', 'gla_w_o', 'ffn_w_up', 'ffn_conv_w', 'ffn_conv_b', 'ffn_w_down', 'ln_mix_g', 'ln_mix_b', 'ln_ffn_g', 'ln_ffn_b', 'loss_target', 'm_sb_w_qkv', 'm_sb_w_o', 'm_gla_w_in', 'm_gla_w_a1', 'm_gla_w_a2', 'm_gla_b_a', 'm_gla_norm_g', 'm_gla_w_o', 'm_ffn_w_up', 'm_ffn_conv_w', 'm_ffn_conv_b', 'm_ffn_w_down', 'm_ln_mix_g', 'm_ln_mix_b', 'm_ln_ffn_g', 'm_ln_ffn_b', 'v_sb_w_qkv', 'v_sb_w_o', 'v_gla_w_in', 'v_gla_w_a1', 'v_gla_w_a2', 'v_gla_b_a', 'v_gla_norm_g', 'v_gla_w_o', 'v_ffn_w_up', 'v_ffn_conv_w', 'v_ffn_conv_b', 'v_ffn_w_down', 'v_ln_mix_g', 'v_ln_mix_b', 'v_ln_ffn_g', 'v_ln_ffn_b']
TWIN_OUTPUTS = ['loss', 'grad_x', 'grad_sb_w_qkv', 'grad_sb_w_o', 'grad_gla_w_in', 'grad_gla_w_a1', 'grad_gla_w_a2', 'grad_gla_b_a', 'grad_gla_norm_g', 'grad_gla_w_o', 'grad_ffn_w_up', 'grad_ffn_conv_w', 'grad_ffn_conv_b', 'grad_ffn_w_down', 'grad_ln_mix_g', 'grad_ln_mix_b', 'grad_ln_ffn_g', 'grad_ln_ffn_b', 'delta_sb_w_qkv', 'delta_sb_w_o', 'delta_gla_w_in', 'delta_gla_w_a1', 'delta_gla_w_a2', 'delta_gla_b_a', 'delta_gla_norm_g', 'delta_gla_w_o', 'delta_ffn_w_up', 'delta_ffn_conv_w', 'delta_ffn_conv_b', 'delta_ffn_w_down', 'delta_ln_mix_g', 'delta_ln_mix_b', 'delta_ln_ffn_g', 'delta_ln_ffn_b', 'new_m_sb_w_qkv', 'new_m_sb_w_o', 'new_m_gla_w_in', 'new_m_gla_w_a1', 'new_m_gla_w_a2', 'new_m_gla_b_a', 'new_m_gla_norm_g', 'new_m_gla_w_o', 'new_m_ffn_w_up', 'new_m_ffn_conv_w', 'new_m_ffn_conv_b', 'new_m_ffn_w_down', 'new_m_ln_mix_g', 'new_m_ln_mix_b', 'new_m_ln_ffn_g', 'new_m_ln_ffn_b', 'new_v_sb_w_qkv', 'new_v_sb_w_o', 'new_v_gla_w_in', 'new_v_gla_w_a1', 'new_v_gla_w_a2', 'new_v_gla_b_a', 'new_v_gla_norm_g', 'new_v_gla_w_o', 'new_v_ffn_w_up', 'new_v_ffn_conv_w', 'new_v_ffn_conv_b', 'new_v_ffn_w_down', 'new_v_ln_mix_g', 'new_v_ln_mix_b', 'new_v_ln_ffn_g', 'new_v_ln_ffn_b']
TWIN_LEAF_KINDS = {'loss': 'loss', 'grad_x': 'grad_x', 'grad_sb_w_qkv': 'grad_w', 'grad_sb_w_o': 'grad_w', 'grad_gla_w_in': 'grad_w', 'grad_gla_w_a1': 'grad_w', 'grad_gla_w_a2': 'grad_w', 'grad_gla_b_a': 'grad_w', 'grad_gla_norm_g': 'grad_w', 'grad_gla_w_o': 'grad_w', 'grad_ffn_w_up': 'grad_w', 'grad_ffn_conv_w': 'grad_w', 'grad_ffn_conv_b': 'grad_w', 'grad_ffn_w_down': 'grad_w', 'grad_ln_mix_g': 'grad_w', 'grad_ln_mix_b': 'grad_w', 'grad_ln_ffn_g': 'grad_w', 'grad_ln_ffn_b': 'grad_w', 'delta_sb_w_qkv': 'delta_w', 'delta_sb_w_o': 'delta_w', 'delta_gla_w_in': 'delta_w', 'delta_gla_w_a1': 'delta_w', 'delta_gla_w_a2': 'delta_w', 'delta_gla_b_a': 'delta_w', 'delta_gla_norm_g': 'delta_w', 'delta_gla_w_o': 'delta_w', 'delta_ffn_w_up': 'delta_w', 'delta_ffn_conv_w': 'delta_w', 'delta_ffn_conv_b': 'delta_w', 'delta_ffn_w_down': 'delta_w', 'delta_ln_mix_g': 'delta_w', 'delta_ln_mix_b': 'delta_w', 'delta_ln_ffn_g': 'delta_w', 'delta_ln_ffn_b': 'delta_w', 'new_m_sb_w_qkv': 'new_m', 'new_m_sb_w_o': 'new_m', 'new_m_gla_w_in': 'new_m', 'new_m_gla_w_a1': 'new_m', 'new_m_gla_w_a2': 'new_m', 'new_m_gla_b_a': 'new_m', 'new_m_gla_norm_g': 'new_m', 'new_m_gla_w_o': 'new_m', 'new_m_ffn_w_up': 'new_m', 'new_m_ffn_conv_w': 'new_m', 'new_m_ffn_conv_b': 'new_m', 'new_m_ffn_w_down': 'new_m', 'new_m_ln_mix_g': 'new_m', 'new_m_ln_mix_b': 'new_m', 'new_m_ln_ffn_g': 'new_m', 'new_m_ln_ffn_b': 'new_m', 'new_v_sb_w_qkv': 'new_v', 'new_v_sb_w_o': 'new_v', 'new_v_gla_w_in': 'new_v', 'new_v_gla_w_a1': 'new_v', 'new_v_gla_w_a2': 'new_v', 'new_v_gla_b_a': 'new_v', 'new_v_gla_norm_g': 'new_v', 'new_v_gla_w_o': 'new_v', 'new_v_ffn_w_up': 'new_v', 'new_v_ffn_conv_w': 'new_v', 'new_v_ffn_conv_b': 'new_v', 'new_v_ffn_w_down': 'new_v', 'new_v_ln_mix_g': 'new_v', 'new_v_ln_mix_b': 'new_v', 'new_v_ln_ffn_g': 'new_v', 'new_v_ln_ffn_b': 'new_v'}


def _forward(args):
    return _fwd_reference(*[args[k] for k in FWD_PARAMS])


def _output_shape():
    out = _jax.eval_shape(lambda: _forward(_fwd_setup_inputs(0)))
    return out.shape, out.dtype

N_MICROBATCH = 1
ADAM_LR = 0.001
ADAM_B1 = 0.9
ADAM_B2 = 0.999
ADAM_EPS = 1e-08
ADAM_WD = 0.01
ADAM_STEP = 10
PER_EXAMPLE_BATCH_AXIS = {'x': 0, 'loss_target': 0}
SHARED_INPUTS = []
_WEIGHT_DTYPES = {'sb_w_qkv': _jnp.float32, 'sb_w_o': _jnp.float32, 'gla_w_in': _jnp.float32, 'gla_w_a1': _jnp.float32, 'gla_w_a2': _jnp.float32, 'gla_b_a': _jnp.float32, 'gla_norm_g': _jnp.float32, 'gla_w_o': _jnp.float32, 'ffn_w_up': _jnp.float32, 'ffn_conv_w': _jnp.float32, 'ffn_conv_b': _jnp.float32, 'ffn_w_down': _jnp.float32, 'ln_mix_g': _jnp.float32, 'ln_mix_b': _jnp.float32, 'ln_ffn_g': _jnp.float32, 'ln_ffn_b': _jnp.float32}
MOMENT_SCALE = {'sb_w_qkv': 2.544863e-02, 'sb_w_o': 4.250654e-02, 'gla_w_in': 6.433818e-02, 'gla_w_a1': 3.305256e-02, 'gla_w_a2': 7.346121e-03, 'gla_b_a': 3.522995e-02, 'gla_norm_g': 8.863030e-02, 'gla_w_o': 8.925657e-02, 'ffn_w_up': 9.025710e-03, 'ffn_conv_w': 3.762637e-03, 'ffn_conv_b': 8.837755e-03, 'ffn_w_down': 1.475809e-02, 'ln_mix_g': 7.442027e-01, 'ln_mix_b': 4.050260e-01, 'ln_ffn_g': 3.196615e+01, 'ln_ffn_b': 6.212583e-01}


def _to_microbatches(a, axis):
    t = _jnp.moveaxis(a, axis, 0)
    t = t.reshape((N_MICROBATCH, t.shape[0] // N_MICROBATCH) + t.shape[1:])
    return _jnp.moveaxis(t, 1, axis + 1)


def setup_inputs(seed: int = 0) -> dict:
    inp = _fwd_setup_inputs(seed)
    key = _jax.random.fold_in(_jax.random.key(seed), 7919)
    shape, _ = _output_shape()
    out = dict(inp)
    out["loss_target"] = _jax.random.normal(_jax.random.fold_in(key, 0), shape, _jnp.float32)
    for i, name in enumerate(TWIN_WEIGHTS):
        w = inp[name].astype(_jnp.float32)
        if MOMENT_SCALE is None:
            s = _jnp.sqrt(_jnp.mean(_jnp.square(w)) + 1e-30)
        else:
            s = MOMENT_SCALE[name]
        km, kv = _jax.random.split(_jax.random.fold_in(key, i + 1))
        out[name] = w
        out["m_" + name] = s * _jax.random.normal(km, w.shape, _jnp.float32)
        out["v_" + name] = (s * s) * _jax.random.uniform(kv, w.shape, _jnp.float32, 0.5, 1.5)
    if N_MICROBATCH > 1:
        for name, axis in PER_EXAMPLE_BATCH_AXIS.items():
            out[name] = _to_microbatches(out[name], axis)
    return {'x': out['x'], 'sb_w_qkv': out['sb_w_qkv'], 'sb_w_o': out['sb_w_o'], 'gla_w_in': out['gla_w_in'], 'gla_w_a1': out['gla_w_a1'], 'gla_w_a2': out['gla_w_a2'], 'gla_b_a': out['gla_b_a'], 'gla_norm_g': out['gla_norm_g'], 'gla_w_o': out['gla_w_o'], 'ffn_w_up': out['ffn_w_up'], 'ffn_conv_w': out['ffn_conv_w'], 'ffn_conv_b': out['ffn_conv_b'], 'ffn_w_down': out['ffn_w_down'], 'ln_mix_g': out['ln_mix_g'], 'ln_mix_b': out['ln_mix_b'], 'ln_ffn_g': out['ln_ffn_g'], 'ln_ffn_b': out['ln_ffn_b'], 'loss_target': out['loss_target'], 'm_sb_w_qkv': out['m_sb_w_qkv'], 'm_sb_w_o': out['m_sb_w_o'], 'm_gla_w_in': out['m_gla_w_in'], 'm_gla_w_a1': out['m_gla_w_a1'], 'm_gla_w_a2': out['m_gla_w_a2'], 'm_gla_b_a': out['m_gla_b_a'], 'm_gla_norm_g': out['m_gla_norm_g'], 'm_gla_w_o': out['m_gla_w_o'], 'm_ffn_w_up': out['m_ffn_w_up'], 'm_ffn_conv_w': out['m_ffn_conv_w'], 'm_ffn_conv_b': out['m_ffn_conv_b'], 'm_ffn_w_down': out['m_ffn_w_down'], 'm_ln_mix_g': out['m_ln_mix_g'], 'm_ln_mix_b': out['m_ln_mix_b'], 'm_ln_ffn_g': out['m_ln_ffn_g'], 'm_ln_ffn_b': out['m_ln_ffn_b'], 'v_sb_w_qkv': out['v_sb_w_qkv'], 'v_sb_w_o': out['v_sb_w_o'], 'v_gla_w_in': out['v_gla_w_in'], 'v_gla_w_a1': out['v_gla_w_a1'], 'v_gla_w_a2': out['v_gla_w_a2'], 'v_gla_b_a': out['v_gla_b_a'], 'v_gla_norm_g': out['v_gla_norm_g'], 'v_gla_w_o': out['v_gla_w_o'], 'v_ffn_w_up': out['v_ffn_w_up'], 'v_ffn_conv_w': out['v_ffn_conv_w'], 'v_ffn_conv_b': out['v_ffn_conv_b'], 'v_ffn_w_down': out['v_ffn_w_down'], 'v_ln_mix_g': out['v_ln_mix_g'], 'v_ln_mix_b': out['v_ln_mix_b'], 'v_ln_ffn_g': out['v_ln_ffn_g'], 'v_ln_ffn_b': out['v_ln_ffn_b']}


def _loss(weights, diff, rest, loss_target):
    with _jax.named_scope("forward"):
        args = {**rest, TWIN_DIFF_INPUT: diff, **{k: w.astype(_WEIGHT_DTYPES[k]) for k, w in weights.items()}}
        y = _forward(args)
    with _jax.named_scope("loss_head"):
        err = _jnp.square(y.astype(_jnp.float32) - loss_target)
        return 0.5 * _jnp.sum(_jnp.mean(err, axis=-1)) if err.ndim else 0.5 * err


def _adamw(w, g, m, v):
    m = ADAM_B1 * m + (1.0 - ADAM_B1) * g
    v = ADAM_B2 * v + (1.0 - ADAM_B2) * _jnp.square(g)
    m_hat = m / (1.0 - ADAM_B1 ** ADAM_STEP)
    v_hat = v / (1.0 - ADAM_B2 ** ADAM_STEP)
    delta = -ADAM_LR * (m_hat / (_jnp.sqrt(v_hat) + ADAM_EPS) + ADAM_WD * w)
    return delta, m, v


def reference(x, sb_w_qkv, sb_w_o, gla_w_in, gla_w_a1, gla_w_a2, gla_b_a, gla_norm_g, gla_w_o, ffn_w_up, ffn_conv_w, ffn_conv_b, ffn_w_down, ln_mix_g, ln_mix_b, ln_ffn_g, ln_ffn_b, loss_target, m_sb_w_qkv, m_sb_w_o, m_gla_w_in, m_gla_w_a1, m_gla_w_a2, m_gla_b_a, m_gla_norm_g, m_gla_w_o, m_ffn_w_up, m_ffn_conv_w, m_ffn_conv_b, m_ffn_w_down, m_ln_mix_g, m_ln_mix_b, m_ln_ffn_g, m_ln_ffn_b, v_sb_w_qkv, v_sb_w_o, v_gla_w_in, v_gla_w_a1, v_gla_w_a2, v_gla_b_a, v_gla_norm_g, v_gla_w_o, v_ffn_w_up, v_ffn_conv_w, v_ffn_conv_b, v_ffn_w_down, v_ln_mix_g, v_ln_mix_b, v_ln_ffn_g, v_ln_ffn_b):
    given = dict(x=x, sb_w_qkv=sb_w_qkv, sb_w_o=sb_w_o, gla_w_in=gla_w_in, gla_w_a1=gla_w_a1, gla_w_a2=gla_w_a2, gla_b_a=gla_b_a, gla_norm_g=gla_norm_g, gla_w_o=gla_w_o, ffn_w_up=ffn_w_up, ffn_conv_w=ffn_conv_w, ffn_conv_b=ffn_conv_b, ffn_w_down=ffn_w_down, ln_mix_g=ln_mix_g, ln_mix_b=ln_mix_b, ln_ffn_g=ln_ffn_g, ln_ffn_b=ln_ffn_b, loss_target=loss_target, m_sb_w_qkv=m_sb_w_qkv, m_sb_w_o=m_sb_w_o, m_gla_w_in=m_gla_w_in, m_gla_w_a1=m_gla_w_a1, m_gla_w_a2=m_gla_w_a2, m_gla_b_a=m_gla_b_a, m_gla_norm_g=m_gla_norm_g, m_gla_w_o=m_gla_w_o, m_ffn_w_up=m_ffn_w_up, m_ffn_conv_w=m_ffn_conv_w, m_ffn_conv_b=m_ffn_conv_b, m_ffn_w_down=m_ffn_w_down, m_ln_mix_g=m_ln_mix_g, m_ln_mix_b=m_ln_mix_b, m_ln_ffn_g=m_ln_ffn_g, m_ln_ffn_b=m_ln_ffn_b, v_sb_w_qkv=v_sb_w_qkv, v_sb_w_o=v_sb_w_o, v_gla_w_in=v_gla_w_in, v_gla_w_a1=v_gla_w_a1, v_gla_w_a2=v_gla_w_a2, v_gla_b_a=v_gla_b_a, v_gla_norm_g=v_gla_norm_g, v_gla_w_o=v_gla_w_o, v_ffn_w_up=v_ffn_w_up, v_ffn_conv_w=v_ffn_conv_w, v_ffn_conv_b=v_ffn_conv_b, v_ffn_w_down=v_ffn_w_down, v_ln_mix_g=v_ln_mix_g, v_ln_mix_b=v_ln_mix_b, v_ln_ffn_g=v_ln_ffn_g, v_ln_ffn_b=v_ln_ffn_b)
    weights = {n: given[n] for n in TWIN_WEIGHTS}
    shared = {n: given[n] for n in SHARED_INPUTS}
    per_example = {n: given[n] for n in ['x']}
    grad_fn = _jax.value_and_grad(_loss, argnums=(0, 1))

    def one_microbatch(ex, loss_target):
        ex = dict(ex)
        diff = ex.pop(TWIN_DIFF_INPUT)
        return grad_fn(weights, diff, {**shared, **ex}, loss_target)

    if N_MICROBATCH == 1:
        loss, (grad_w, grad_x) = one_microbatch(per_example, given["loss_target"])
    else:
        def body(carry, xs):
            loss_sum, grad_sum = carry
            l_k, (gw_k, gx_k) = one_microbatch(xs[0], xs[1])
            with _jax.named_scope("update"):
                return (loss_sum + l_k, _jax.tree.map(_jnp.add, grad_sum, gw_k)), gx_k

        init = (_jnp.zeros((), _jnp.float32), _jax.tree.map(_jnp.zeros_like, weights))
        (loss, grad_w), grad_x = _jax.lax.scan(body, init, (per_example, given["loss_target"]))
    with _jax.named_scope("update"):
        delta_w, new_m, new_v = {}, {}, {}
        for n in TWIN_WEIGHTS:
            delta_w[n], new_m[n], new_v[n] = _adamw(weights[n], grad_w[n], given["m_" + n], given["v_" + n])
    return (loss, grad_x, *[grad_w[n] for n in TWIN_WEIGHTS], *[delta_w[n] for n in TWIN_WEIGHTS],
            *[new_m[n] for n in TWIN_WEIGHTS], *[new_v[n] for n in TWIN_WEIGHTS])
```

```python
import functools
import math

import jax
import jax.numpy as jnp
from jax import lax
from jax.experimental import pallas as pl
from jax.experimental.pallas import tpu as pltpu

F32 = jnp.float32
BF16 = jnp.bfloat16
MXU_DTYPE = jnp.bfloat16

DEPTH = 4
ALPHA = (2.0 * DEPTH) ** 0.25
SB_HEAD_DIM = 64
GLA_DK = 128
GLA_DV = 256
GLA_GATE_RANK = 16
GLA_GATE_TAU = 16.0
GLA_CHUNK = 64
LN_EPS = 1e-5
ADAM_LR = 0.001
ADAM_B1 = 0.9
ADAM_B2 = 0.999
ADAM_EPS = 1e-08
ADAM_WD = 0.01
ADAM_STEP = 10

LANES = 128
SUBLANES = 8
BF16_ROWS = 16
VMEM_LIMIT = 56 * 1024 * 1024
N_DEV = 8
PACK_COLS = 1024

NN = ((1,), (0,))
NT = ((1,), (1,))
TN = ((0,), (0,))


def _dot(a, b, dims):
    return lax.dot_general(a.astype(MXU_DTYPE), b.astype(MXU_DTYPE), (dims, ((), ())),
                           preferred_element_type=F32)


def _split_dot(x, u, dims):
    hi = x.astype(MXU_DTYPE)
    lo = (x - hi.astype(F32)).astype(MXU_DTYPE)
    return (lax.dot_general(hi, u, (dims, ((), ())), preferred_element_type=F32)
            + lax.dot_general(lo, u, (dims, ((), ())), preferred_element_type=F32))


def _split_dot_left(u, x, dims):
    hi = x.astype(MXU_DTYPE)
    lo = (x - hi.astype(F32)).astype(MXU_DTYPE)
    return (lax.dot_general(u, hi, (dims, ((), ())), preferred_element_type=F32)
            + lax.dot_general(u, lo, (dims, ((), ())), preferred_element_type=F32))


def _sigmoid(x):
    return 1.0 / (1.0 + jnp.exp(-x))


def _log_sigmoid(x):
    return jnp.minimum(x, 0.0) - jnp.log(1.0 + jnp.exp(-jnp.abs(x)))


def _pick(dim, candidates):
    for c in candidates:
        if dim % c == 0:
            return c
    raise ValueError(f"no tile for {dim} in {candidates}")


def _params(sem):
    return pltpu.CompilerParams(dimension_semantics=sem, vmem_limit_bytes=VMEM_LIMIT)


def matmul(a, b, mode, out_dtype, name, resid=None, resid_scale=1.0):
    if mode == "nn":
        (m, k), (k2, n) = a.shape, b.shape
    elif mode == "nt":
        (m, k), (n, k2) = a.shape, b.shape
    else:
        (k, m), (k2, n) = a.shape, b.shape
    assert k == k2, (a.shape, b.shape, mode)
    tm = _pick(m, (1024, 512, 256, 128))
    tn = _pick(n, (1024, 704 * 2, 640, 512, 256, 128))
    tk = _pick(k, (1024, 704 * 2, 640, 512, 256, 128))
    nk = k // tk
    dims = {"nn": NN, "nt": NT, "tn": TN}[mode]

    if mode == "tn":
        a_spec = pl.BlockSpec((tk, tm), lambda i, j, kk: (kk, i))
    else:
        a_spec = pl.BlockSpec((tm, tk), lambda i, j, kk: (i, kk))
    if mode == "nt":
        b_spec = pl.BlockSpec((tn, tk), lambda i, j, kk: (j, kk))
    else:
        b_spec = pl.BlockSpec((tk, tn), lambda i, j, kk: (kk, j))
    o_spec = pl.BlockSpec((tm, tn), lambda i, j, kk: (i, j))
    has_resid = resid is not None

    def body(*refs):
        if has_resid:
            a_ref, b_ref, r_ref, o_ref, acc_ref = refs
        else:
            a_ref, b_ref, o_ref, acc_ref = refs
        kk = pl.program_id(2)

        @pl.when(kk == 0)
        def _():
            acc_ref[...] = jnp.zeros_like(acc_ref)

        acc_ref[...] += _dot(a_ref[...], b_ref[...], dims)

        @pl.when(kk == nk - 1)
        def _():
            acc = acc_ref[...]
            if has_resid:
                acc = acc + resid_scale * r_ref[...]
            o_ref[...] = acc.astype(out_dtype)

    in_specs = [a_spec, b_spec] + ([o_spec] if has_resid else [])
    args = (a, b) + ((resid,) if has_resid else ())
    return pl.pallas_call(
        body, name=name, grid=(m // tm, n // tn, nk),
        in_specs=in_specs, out_specs=o_spec,
        out_shape=jax.ShapeDtypeStruct((m, n), out_dtype),
        scratch_shapes=[pltpu.VMEM((tm, tn), F32)],
        compiler_params=_params(("parallel", "parallel", "arbitrary")),
    )(*args)


def resid_ln_fwd(x, y, g, b, name):
    t, d = x.shape
    tm = _pick(t, (256, 128))

    def body(x_ref, y_ref, g_ref, b_ref, o_ref, ob_ref, s_ref):
        s = ALPHA * x_ref[...] + y_ref[...]
        mu = jnp.mean(s, axis=-1, keepdims=True)
        c = s - mu
        var = jnp.mean(c * c, axis=-1, keepdims=True)
        out = c * lax.rsqrt(var + LN_EPS) * g_ref[...] + b_ref[...]
        s_ref[...] = s
        o_ref[...] = out
        ob_ref[...] = out.astype(MXU_DTYPE)

    row = pl.BlockSpec((tm, d), lambda i: (i, 0))
    vec = pl.BlockSpec((1, d), lambda i: (0, 0))
    return pl.pallas_call(
        body, name=name, grid=(t // tm,),
        in_specs=[row, row, vec, vec], out_specs=[row, row, row],
        out_shape=[jax.ShapeDtypeStruct((t, d), F32), jax.ShapeDtypeStruct((t, d), MXU_DTYPE),
                   jax.ShapeDtypeStruct((t, d), F32)],
        compiler_params=_params(("parallel",)),
    )(x, y, g.reshape(1, d), b.reshape(1, d))


def ln_bwd(dout, s, g, name):
    t, d = s.shape
    tm = _pick(t, (256, 128))

    def body(do_ref, s_ref, g_ref, ds_ref, dsb_ref, dg_ref, db_ref):
        @pl.when(pl.program_id(0) == 0)
        def _():
            dg_ref[...] = jnp.zeros_like(dg_ref)
            db_ref[...] = jnp.zeros_like(db_ref)

        sv = s_ref[...]
        mu = jnp.mean(sv, axis=-1, keepdims=True)
        c = sv - mu
        var = jnp.mean(c * c, axis=-1, keepdims=True)
        rstd = lax.rsqrt(var + LN_EPS)
        xhat = c * rstd
        do = do_ref[...]
        dxh = do * g_ref[...]
        m1 = jnp.mean(dxh, axis=-1, keepdims=True)
        m2 = jnp.mean(dxh * xhat, axis=-1, keepdims=True)
        ds = rstd * (dxh - m1 - xhat * m2)
        ds_ref[...] = ds
        dsb_ref[...] = ds.astype(MXU_DTYPE)
        dg_ref[...] += jnp.sum(do * xhat, axis=0, keepdims=True)
        db_ref[...] += jnp.sum(do, axis=0, keepdims=True)

    row = pl.BlockSpec((tm, d), lambda i: (i, 0))
    vec = pl.BlockSpec((1, d), lambda i: (0, 0))
    return pl.pallas_call(
        body, name=name, grid=(t // tm,),
        in_specs=[row, row, vec], out_specs=[row, row, vec, vec],
        out_shape=[jax.ShapeDtypeStruct((t, d), F32), jax.ShapeDtypeStruct((t, d), MXU_DTYPE),
                   jax.ShapeDtypeStruct((1, d), F32), jax.ShapeDtypeStruct((1, d), F32)],
        compiler_params=_params(("arbitrary",)),
    )(dout, s, g.reshape(1, d))


def loss_and_grad(y, target, name):
    t, d = y.shape
    tm = _pick(t, (256, 128))

    def body(y_ref, t_ref, l_ref, dy_ref):
        @pl.when(pl.program_id(0) == 0)
        def _():
            l_ref[...] = jnp.zeros_like(l_ref)

        e = y_ref[...] - t_ref[...]
        dy_ref[...] = e / d
        l_ref[...] += jnp.sum(e * e, axis=0, keepdims=True) * (0.5 / d)

    row = pl.BlockSpec((tm, d), lambda i: (i, 0))
    vec = pl.BlockSpec((1, d), lambda i: (0, 0))
    return pl.pallas_call(
        body, name=name, grid=(t // tm,),
        in_specs=[row, row], out_specs=[vec, row],
        out_shape=[jax.ShapeDtypeStruct((1, d), F32), jax.ShapeDtypeStruct((t, d), F32)],
        compiler_params=_params(("arbitrary",)),
    )(y, target)


SB_TQ = 512
SB_TK = 128


def _sb_block(qa, kj, run_l, upper, diag_mask):
    z = _dot(qa, kj, NT)
    sp = jnp.log(1.0 + jnp.exp(-jnp.abs(z)))
    ls_pos = jnp.minimum(z, 0.0) - sp
    ls_neg = -jnp.maximum(z, 0.0) - sp
    if diag_mask is not None:
        ls_neg = jnp.where(diag_mask, ls_neg, 0.0)
    tail = _split_dot(ls_neg, upper, NN) + run_l
    a = jnp.exp(ls_pos + tail)
    if diag_mask is not None:
        a = jnp.where(diag_mask, a, 0.0)
    return z, ls_pos, ls_neg, a


def sb_attention_fwd(qkv, name):
    t, d3 = qkv.shape
    d = d3 // 3
    npair = d // LANES
    tq, tk = min(SB_TQ, t), SB_TK
    nq, ratio = t // tq, tq // tk
    scale = SB_HEAD_DIM ** -0.5

    def body(q_ref, k_ref, v_ref, o_ref, of_ref, qs_ref, acc_ref, fine_ref, run_ref):
        i = pl.program_id(1)
        lane = lax.broadcasted_iota(jnp.int32, (1, LANES), 1)
        upper = (lax.broadcasted_iota(jnp.int32, (tk, tk), 0)
                 > lax.broadcasted_iota(jnp.int32, (tk, tk), 1)).astype(MXU_DTYPE)
        q2 = q_ref[...]
        for a in range(2):
            qs_ref[a] = jnp.where(lane // SB_HEAD_DIM == a, q2, jnp.zeros_like(q2)) * scale
        acc_ref[...] = jnp.zeros_like(acc_ref)
        fine_ref[...] = jnp.zeros_like(fine_ref)
        run_ref[...] = jnp.zeros_like(run_ref)

        def block(start, row0, mask):
            kj = k_ref[pl.ds(start, tk), :]
            vj = v_ref[pl.ds(start, tk), :]
            for a in range(2):
                _, _, ls_neg, p = _sb_block(qs_ref[a, row0:, :], kj, run_ref[a, row0:, :], upper, mask)
                p_mx = p.astype(MXU_DTYPE)
                acc_ref[a, row0:, :] += _dot(p_mx, vj, NN)
                fine_ref[a, row0:, :] += _dot(p - p_mx.astype(F32), vj, NN)
                run_ref[a, row0:, :] += jnp.sum(ls_neg, axis=-1, keepdims=True)

        for back in range(ratio):
            row0 = (ratio - 1 - back) * tk
            rows = tq - row0
            mask = (lax.broadcasted_iota(jnp.int32, (rows, tk), 1)
                    < lax.broadcasted_iota(jnp.int32, (rows, tk), 0))
            block(pl.multiple_of(i * tq + row0, tk), row0, mask)

        def earlier(n, _):
            block(pl.multiple_of((i * ratio - 1 - n) * tk, tk), 0, None)
            return 0

        lax.fori_loop(0, i * ratio, earlier, 0)
        o = jnp.where(lane < SB_HEAD_DIM, acc_ref[0], acc_ref[1])
        of_ref[...] = o + jnp.where(lane < SB_HEAD_DIM, fine_ref[0], fine_ref[1])
        o_ref[...] = o.astype(MXU_DTYPE)

    q_spec = pl.BlockSpec((tq, LANES), lambda p, i: (i, p))
    k_spec = pl.BlockSpec((t, LANES), lambda p, i: (0, npair + p))
    v_spec = pl.BlockSpec((t, LANES), lambda p, i: (0, 2 * npair + p))
    o_spec = pl.BlockSpec((tq, LANES), lambda p, i: (i, p))
    return pl.pallas_call(
        body, name=name, grid=(npair, nq),
        in_specs=[q_spec, k_spec, v_spec], out_specs=[o_spec, o_spec],
        out_shape=[jax.ShapeDtypeStruct((t, d), MXU_DTYPE), jax.ShapeDtypeStruct((t, d), F32)],
        scratch_shapes=[pltpu.VMEM((2, tq, LANES), MXU_DTYPE), pltpu.VMEM((2, tq, LANES), F32),
                        pltpu.VMEM((2, tq, LANES), F32), pltpu.VMEM((2, tq, 1), F32)],
        compiler_params=_params(("parallel", "arbitrary")),
    )(qkv, qkv, qkv)


def sb_attention_bwd(qkv, do, o_f32, name):
    t, d3 = qkv.shape
    d = d3 // 3
    npair = d // LANES
    tq, tk = min(SB_TQ, t), SB_TK
    nq, ratio = t // tq, tq // tk
    scale = SB_HEAD_DIM ** -0.5

    def body(q_ref, k_ref, v_ref, do_ref, of_ref, dq_ref, dk_ref, dv_ref,
             dk_acc, dv_acc, qs_ref, dos_ref, dq_acc, etot_ref, runl_ref, rune_ref):
        i = pl.program_id(1)

        @pl.when(i == 0)
        def _():
            dk_acc[...] = jnp.zeros_like(dk_acc)
            dv_acc[...] = jnp.zeros_like(dv_acc)

        lane = lax.broadcasted_iota(jnp.int32, (1, LANES), 1)
        r_i = lax.broadcasted_iota(jnp.int32, (tk, tk), 0)
        c_i = lax.broadcasted_iota(jnp.int32, (tk, tk), 1)
        upper = (r_i > c_i).astype(MXU_DTYPE)
        upper_incl = (r_i >= c_i).astype(MXU_DTYPE)
        q2 = q_ref[...]
        do2 = do_ref[...]
        prod = do2.astype(F32) * of_ref[...]
        for a in range(2):
            head = lane // SB_HEAD_DIM == a
            qs_ref[a] = jnp.where(head, q2, jnp.zeros_like(q2)) * scale
            dos_ref[a] = jnp.where(head, do2, jnp.zeros_like(do2))
            etot_ref[a] = jnp.sum(jnp.where(head, prod, 0.0), axis=-1, keepdims=True)
        dq_acc[...] = jnp.zeros_like(dq_acc)
        runl_ref[...] = jnp.zeros_like(runl_ref)
        rune_ref[...] = jnp.zeros_like(rune_ref)

        def block(start, row0, mask):
            kj = k_ref[pl.ds(start, tk), :]
            vj = v_ref[pl.ds(start, tk), :]
            dk_blk = jnp.zeros((tk, LANES), F32)
            dv_blk = jnp.zeros((tk, LANES), F32)
            for a in range(2):
                qa = qs_ref[a, row0:, :]
                doa = dos_ref[a, row0:, :]
                z, ls_pos, ls_neg, p = _sb_block(qa, kj, runl_ref[a, row0:, :], upper, mask)
                beta = jnp.exp(ls_pos)
                e = p * _dot(doa, vj, NT)
                q_incl = _split_dot(e, upper_incl, NN) + rune_ref[a, row0:, :]
                dz = e * (1.0 - beta) - beta * (etot_ref[a, row0:, :] - q_incl)
                if mask is not None:
                    dz = jnp.where(mask, dz, 0.0)
                dz = dz.astype(MXU_DTYPE)
                dq_acc[a, row0:, :] += _dot(dz, kj, NN)
                dk_blk = dk_blk + _dot(dz, qa, TN)
                dv_blk = dv_blk + _dot(p, doa, TN)
                runl_ref[a, row0:, :] += jnp.sum(ls_neg, axis=-1, keepdims=True)
                rune_ref[a, row0:, :] += jnp.sum(e, axis=-1, keepdims=True)
            dk_acc[pl.ds(start, tk), :] += dk_blk
            dv_acc[pl.ds(start, tk), :] += dv_blk

        for back in range(ratio):
            row0 = (ratio - 1 - back) * tk
            rows = tq - row0
            mask = (lax.broadcasted_iota(jnp.int32, (rows, tk), 1)
                    < lax.broadcasted_iota(jnp.int32, (rows, tk), 0))
            block(pl.multiple_of(i * tq + row0, tk), row0, mask)

        def earlier(n, _):
            block(pl.multiple_of((i * ratio - 1 - n) * tk, tk), 0, None)
            return 0

        lax.fori_loop(0, i * ratio, earlier, 0)
        dq = jnp.where(lane < SB_HEAD_DIM, dq_acc[0], dq_acc[1]) * scale
        dq_ref[...] = dq.astype(MXU_DTYPE)

        @pl.when(i == nq - 1)
        def _():
            dk_ref[...] = dk_acc[...].astype(MXU_DTYPE)
            dv_ref[...] = dv_acc[...].astype(MXU_DTYPE)

    q_spec = pl.BlockSpec((tq, LANES), lambda p, i: (i, p))
    k_spec = pl.BlockSpec((t, LANES), lambda p, i: (0, npair + p))
    v_spec = pl.BlockSpec((t, LANES), lambda p, i: (0, 2 * npair + p))
    full = pl.BlockSpec((t, LANES), lambda p, i: (0, p))
    return pl.pallas_call(
        body, name=name, grid=(npair, nq),
        in_specs=[q_spec, k_spec, v_spec, q_spec, q_spec], out_specs=[q_spec, full, full],
        out_shape=[jax.ShapeDtypeStruct((t, d), MXU_DTYPE)] * 3,
        scratch_shapes=[pltpu.VMEM((t, LANES), F32), pltpu.VMEM((t, LANES), F32),
                        pltpu.VMEM((2, tq, LANES), MXU_DTYPE), pltpu.VMEM((2, tq, LANES), MXU_DTYPE),
                        pltpu.VMEM((2, tq, LANES), F32), pltpu.VMEM((2, tq, 1), F32),
                        pltpu.VMEM((2, tq, 1), F32), pltpu.VMEM((2, tq, 1), F32)],
        compiler_params=_params(("parallel", "arbitrary")),
    )(qkv, qkv, qkv, do, o_f32)


GLA_ROWS = 256


def _gla_cols(heads):
    kd, vd = heads * GLA_DK, heads * GLA_DV
    return kd, vd, 2 * kd, 2 * kd + vd, 2 * kd + 2 * vd


def _gla_chunk(proj_ref, rows, h, heads, wa2, ba):
    kd, vd, v0, r0, g0 = _gla_cols(heads)
    c = GLA_CHUNK
    q = proj_ref[rows, h * GLA_DK:(h + 1) * GLA_DK]
    k = proj_ref[rows, kd + h * GLA_DK:kd + (h + 1) * GLA_DK]
    v = proj_ref[rows, v0 + h * GLA_DV:v0 + (h + 1) * GLA_DV]
    xa = proj_ref[rows, g0:g0 + LANES]
    gp = _dot(xa, wa2, NN) + ba
    g = _log_sigmoid(gp) / GLA_GATE_TAU
    r_i = lax.broadcasted_iota(jnp.int32, (c, c), 0)
    c_i = lax.broadcasted_iota(jnp.int32, (c, c), 1)
    causal = c_i <= r_i
    b = _split_dot_left(causal.astype(MXU_DTYPE), g, NN)
    bl = b[c - 1:c, :]
    qd = q * (GLA_DK ** -0.5) * jnp.exp(b)
    ki = k * jnp.exp(-b)
    kdec = k * jnp.exp(bl - b)
    attn = jnp.where(causal, _dot(qd, ki, NT), 0.0)
    return q, k, v, xa, gp, b, bl, qd, ki, kdec, attn, causal


def gla_fwd(proj, wa2, ba, norm_g, heads, name):
    t, width = proj.shape
    kd, vd, v0, r0, g0 = _gla_cols(heads)
    assert width == g0 + LANES
    c = GLA_CHUNK
    rg = min(GLA_ROWS, t)
    ncs = rg // c

    def body(proj_ref, wa2_ref, ba_ref, ng_ref, og_ref, o_ref, st_ref, state):
        @pl.when(pl.program_id(0) == 0)
        def _():
            state[...] = jnp.zeros_like(state)

        ng = ng_ref[...]
        for h in range(heads):
            wa2_h = wa2_ref[:, h * GLA_DK:(h + 1) * GLA_DK]
            ba_h = ba_ref[:, h * GLA_DK:(h + 1) * GLA_DK]

            def chunk(n, _):
                rows = pl.ds(pl.multiple_of(n * c, c), c)
                q, k, v, xa, gp, b, bl, qd, ki, kdec, attn, _ = _gla_chunk(proj_ref, rows, h, heads, wa2_h, ba_h)
                s_t = state[h]
                st_ref[n, h] = s_t
                o = _dot(attn, v, NN) + _dot(qd, s_t, NT)
                state[h] = s_t * jnp.exp(bl) + _dot(v, kdec, TN)
                o_ref[rows, h * GLA_DV:(h + 1) * GLA_DV] = o
                rstd = lax.rsqrt(jnp.mean(o * o, axis=-1, keepdims=True) + LN_EPS)
                r = proj_ref[rows, r0 + h * GLA_DV:r0 + (h + 1) * GLA_DV]
                og = o * rstd * ng * (r * _sigmoid(r))
                og_ref[rows, h * GLA_DV:(h + 1) * GLA_DV] = og.astype(MXU_DTYPE)
                return 0

            lax.fori_loop(0, ncs, chunk, 0)

    return pl.pallas_call(
        body, name=name, grid=(t // rg,),
        in_specs=[pl.BlockSpec((rg, width), lambda i: (i, 0)),
                  pl.BlockSpec((LANES, kd), lambda i: (0, 0)),
                  pl.BlockSpec((1, kd), lambda i: (0, 0)),
                  pl.BlockSpec((1, GLA_DV), lambda i: (0, 0))],
        out_specs=[pl.BlockSpec((rg, vd), lambda i: (i, 0)),
                   pl.BlockSpec((rg, vd), lambda i: (i, 0)),
                   pl.BlockSpec((ncs, heads, GLA_DV, GLA_DK), lambda i: (i, 0, 0, 0))],
        out_shape=[jax.ShapeDtypeStruct((t, vd), MXU_DTYPE), jax.ShapeDtypeStruct((t, vd), F32),
                   jax.ShapeDtypeStruct((t // c, heads, GLA_DV, GLA_DK), F32)],
        scratch_shapes=[pltpu.VMEM((heads, GLA_DV, GLA_DK), F32)],
        compiler_params=_params(("arbitrary",)),
    )(proj, wa2, ba, norm_g)


def gla_bwd(proj, wa2, ba, norm_g, o_pre, states, dog, heads, name):
    t, width = proj.shape
    kd, vd, v0, r0, g0 = _gla_cols(heads)
    c = GLA_CHUNK
    rg = min(GLA_ROWS, t)
    ncs = rg // c
    ngrp = t // rg

    def body(proj_ref, wa2_ref, ba_ref, ng_ref, o_ref, st_ref, dog_ref,
             dproj_ref, dwa2_ref, dba_ref, dng_ref, dstate):
        @pl.when(pl.program_id(0) == 0)
        def _():
            dstate[...] = jnp.zeros_like(dstate)
            dwa2_ref[...] = jnp.zeros_like(dwa2_ref)
            dba_ref[...] = jnp.zeros_like(dba_ref)
            dng_ref[...] = jnp.zeros_like(dng_ref)

        ng = ng_ref[...]
        dproj_ref[:, g0:g0 + LANES] = jnp.zeros((rg, LANES), F32)
        for h in range(heads):
            wa2_h = wa2_ref[:, h * GLA_DK:(h + 1) * GLA_DK]
            ba_h = ba_ref[:, h * GLA_DK:(h + 1) * GLA_DK]

            def chunk(m, _):
                n = ncs - 1 - m
                rows = pl.ds(pl.multiple_of(n * c, c), c)
                q, k, v, xa, gp, b, bl, qd, ki, kdec, attn, causal = _gla_chunk(
                    proj_ref, rows, h, heads, wa2_h, ba_h)
                s_t = st_ref[n, h]
                ds_t = dstate[h]
                o = o_ref[rows, h * GLA_DV:(h + 1) * GLA_DV]
                r = proj_ref[rows, r0 + h * GLA_DV:r0 + (h + 1) * GLA_DV]
                dg_out = dog_ref[rows, h * GLA_DV:(h + 1) * GLA_DV]
                rstd = lax.rsqrt(jnp.mean(o * o, axis=-1, keepdims=True) + LN_EPS)
                ohat = o * rstd
                sig = _sigmoid(r)
                silu = r * sig
                dn = dg_out * silu
                dr = dg_out * (ohat * ng) * (sig * (1.0 + r * (1.0 - sig)))
                dng_ref[...] += jnp.sum(dn * ohat, axis=0, keepdims=True)
                dohat = dn * ng
                d_o = rstd * (dohat - ohat * jnp.mean(dohat * ohat, axis=-1, keepdims=True))
                d_attn = jnp.where(causal, _dot(d_o, v, NT), 0.0)
                dv = _dot(attn, d_o, TN) + _dot(kdec, ds_t, NT)
                dqd = _dot(d_attn, ki, NN) + _dot(d_o, s_t, NN)
                dki = _dot(d_attn, qd, TN)
                dkdec = _dot(v, ds_t, NN)
                decay = jnp.exp(bl)
                dbl = jnp.sum(ds_t * s_t, axis=0, keepdims=True) * decay
                dstate[h] = ds_t * decay + _dot(d_o, qd, TN)
                dq = dqd * (GLA_DK ** -0.5) * jnp.exp(b)
                dk = dki * jnp.exp(-b) + dkdec * jnp.exp(bl - b)
                db = dqd * qd - dki * ki - dkdec * kdec
                dbl = dbl + jnp.sum(dkdec * kdec, axis=0, keepdims=True)
                last = lax.broadcasted_iota(jnp.int32, (c, 1), 0) == c - 1
                db = db + jnp.where(last, dbl, 0.0)
                r_i = lax.broadcasted_iota(jnp.int32, (c, c), 0)
                c_i = lax.broadcasted_iota(jnp.int32, (c, c), 1)
                dg = _split_dot_left((c_i >= r_i).astype(MXU_DTYPE), db, NN)
                dgp = dg * (1.0 / GLA_GATE_TAU) * _sigmoid(-gp)
                dproj_ref[rows, h * GLA_DK:(h + 1) * GLA_DK] = dq
                dproj_ref[rows, kd + h * GLA_DK:kd + (h + 1) * GLA_DK] = dk
                dproj_ref[rows, v0 + h * GLA_DV:v0 + (h + 1) * GLA_DV] = dv
                dproj_ref[rows, r0 + h * GLA_DV:r0 + (h + 1) * GLA_DV] = dr
                dproj_ref[rows, g0:g0 + LANES] += _dot(dgp, wa2_h, NT)
                dwa2_ref[:, h * GLA_DK:(h + 1) * GLA_DK] += _dot(xa, dgp, TN)
                dba_ref[:, h * GLA_DK:(h + 1) * GLA_DK] += jnp.sum(dgp, axis=0, keepdims=True)
                return 0

            lax.fori_loop(0, ncs, chunk, 0)

    rev = lambda i: (ngrp - 1 - i, 0)
    return pl.pallas_call(
        body, name=name, grid=(ngrp,),
        in_specs=[pl.BlockSpec((rg, width), rev),
                  pl.BlockSpec((LANES, kd), lambda i: (0, 0)),
                  pl.BlockSpec((1, kd), lambda i: (0, 0)),
                  pl.BlockSpec((1, GLA_DV), lambda i: (0, 0)),
                  pl.BlockSpec((rg, vd), rev),
                  pl.BlockSpec((ncs, heads, GLA_DV, GLA_DK), lambda i: (ngrp - 1 - i, 0, 0, 0)),
                  pl.BlockSpec((rg, vd), rev)],
        out_specs=[pl.BlockSpec((rg, width), rev),
                   pl.BlockSpec((LANES, kd), lambda i: (0, 0)),
                   pl.BlockSpec((1, kd), lambda i: (0, 0)),
                   pl.BlockSpec((1, GLA_DV), lambda i: (0, 0))],
        out_shape=[jax.ShapeDtypeStruct((t, width), F32), jax.ShapeDtypeStruct((LANES, kd), F32),
                   jax.ShapeDtypeStruct((1, kd), F32), jax.ShapeDtypeStruct((1, GLA_DV), F32)],
        scratch_shapes=[pltpu.VMEM((heads, GLA_DV, GLA_DK), F32)],
        compiler_params=_params(("arbitrary",)),
    )(proj, wa2, ba, norm_g, o_pre, states, dog)


CONV_ROWS = 512
CONV_COLS = 256


def _shift_down(xv, halo_rows, nshift):
    rolled = pltpu.roll(xv, nshift, axis=0)
    ridx = lax.broadcasted_iota(jnp.int32, xv.shape, 0)
    out = rolled
    for r in range(nshift):
        src = halo_rows[halo_rows.shape[0] - nshift + r:halo_rows.shape[0] - nshift + r + 1, :]
        out = jnp.where(ridx == r, src, out)
    return out


def _shift_up(xv, halo_rows, nshift):
    n = xv.shape[0]
    rolled = pltpu.roll(xv, n - nshift, axis=0)
    ridx = lax.broadcasted_iota(jnp.int32, xv.shape, 0)
    out = rolled
    for r in range(nshift):
        out = jnp.where(ridx == n - nshift + r, halo_rows[r:r + 1, :], out)
    return out


def _mxu_rounded(v):
    return v.astype(MXU_DTYPE).astype(F32)


def _conv3(hv, prev, cw, cb):
    cw = _mxu_rounded(cw)
    return cw[0:1, :] * _shift_down(hv, prev, 2) + cw[1:2, :] * _shift_down(hv, prev, 1) + cw[2:3, :] * hv + cb


def _conv_specs(t, f):
    tm = _pick(t, (CONV_ROWS, 256, 128))
    tc = _pick(f, (CONV_COLS, 128))
    hb = BF16_ROWS
    nrow, ncol = t // tm, f // tc
    per = tm // hb
    tile = lambda off: pl.BlockSpec((tm, tc), lambda j, i: (i, off + j))
    prev = lambda off: pl.BlockSpec((hb, tc), lambda j, i: (jnp.maximum(i * per - 1, 0), off + j))
    nxt = lambda off: pl.BlockSpec((hb, tc), lambda j, i: (jnp.minimum((i + 1) * per, t // hb - 1), off + j))
    vec = lambda rows, off: pl.BlockSpec((rows, tc), lambda j, i: (0, off + j))
    return tm, tc, nrow, ncol, tile, prev, nxt, vec


def conv_gate_fwd(hu, hg, cwu, cwg, cbu, cbg, name):
    t, f = hu.shape
    tm, tc, nrow, ncol, tile, prev, nxt, vec = _conv_specs(t, f)

    def body(hu_ref, hg_ref, pu_ref, pg_ref, cwu_ref, cwg_ref, cbu_ref, cbg_ref, act_ref):
        first = pl.program_id(1) == 0
        pu = jnp.where(first, 0.0, pu_ref[...].astype(F32))
        pg = jnp.where(first, 0.0, pg_ref[...].astype(F32))
        u = _conv3(hu_ref[...].astype(F32), pu, cwu_ref[...], cbu_ref[...])
        g = _conv3(hg_ref[...].astype(F32), pg, cwg_ref[...], cbg_ref[...])
        act_ref[...] = (g * _sigmoid(g) * u).astype(MXU_DTYPE)

    return pl.pallas_call(
        body, name=name, grid=(ncol, nrow),
        in_specs=[tile(0), tile(0), prev(0), prev(0), vec(3, 0), vec(3, 0), vec(1, 0), vec(1, 0)],
        out_specs=tile(0),
        out_shape=jax.ShapeDtypeStruct((t, f), MXU_DTYPE),
        compiler_params=_params(("parallel", "arbitrary")),
    )(hu, hg, hu, hg, cwu, cwg, cbu, cbg)


def conv_gate_bwd_act(dact, hu, hg, cwu, cwg, cbu, cbg, name):
    t, f = hu.shape
    tm, tc, nrow, ncol, tile, prev, nxt, vec = _conv_specs(t, f)

    def body(da_ref, hu_ref, hg_ref, pu_ref, pg_ref, cwu_ref, cwg_ref, cbu_ref, cbg_ref, du_ref, dg_ref):
        first = pl.program_id(1) == 0
        pu = jnp.where(first, 0.0, pu_ref[...].astype(F32))
        pg = jnp.where(first, 0.0, pg_ref[...].astype(F32))
        u = _conv3(hu_ref[...].astype(F32), pu, cwu_ref[...], cbu_ref[...])
        g = _conv3(hg_ref[...].astype(F32), pg, cwg_ref[...], cbg_ref[...])
        da = da_ref[...].astype(F32)
        sig = _sigmoid(g)
        du_ref[...] = da * (g * sig)
        dg_ref[...] = da * u * (sig * (1.0 + g * (1.0 - sig)))

    out = pl.pallas_call(
        body, name=name, grid=(ncol, nrow),
        in_specs=[tile(0), tile(0), tile(0), prev(0), prev(0),
                  vec(3, 0), vec(3, 0), vec(1, 0), vec(1, 0)],
        out_specs=[tile(0), tile(0)],
        out_shape=[jax.ShapeDtypeStruct((t, f), F32), jax.ShapeDtypeStruct((t, f), F32)],
        compiler_params=_params(("parallel", "arbitrary")),
    )(dact, hu, hg, hu, hg, cwu, cwg, cbu, cbg)
    return out


def conv_bwd(dhc, h, cw, name):
    t, f = dhc.shape
    tm, tc, nrow, ncol, tile, prev, nxt, vec = _conv_specs(t, f)
    hb = BF16_ROWS
    per = tm // hb
    nxt32 = pl.BlockSpec((hb, tc), lambda j, i: (jnp.minimum((i + 1) * per, t // hb - 1), j))

    def body(d_ref, dn_ref, h_ref, hp_ref, cw_ref, dh_ref, dcw_ref, dcb_ref):
        i = pl.program_id(1)

        @pl.when(i == 0)
        def _():
            dcw_ref[...] = jnp.zeros_like(dcw_ref)
            dcb_ref[...] = jnp.zeros_like(dcb_ref)

        dcb_ref[...] += jnp.sum(d_ref[...], axis=0, keepdims=True)
        dv = _mxu_rounded(d_ref[...])
        nx = jnp.where(i == nrow - 1, 0.0, _mxu_rounded(dn_ref[...]))
        hp = jnp.where(i == 0, 0.0, hp_ref[...].astype(F32))
        hv = h_ref[...].astype(F32)
        cwv = _mxu_rounded(cw_ref[...])
        dh = cwv[2:3, :] * dv + cwv[1:2, :] * _shift_up(dv, nx, 1) + cwv[0:1, :] * _shift_up(dv, nx, 2)
        dh_ref[...] = dh.astype(MXU_DTYPE)
        dcw_ref[0:1, :] += jnp.sum(dv * _shift_down(hv, hp, 2), axis=0, keepdims=True)
        dcw_ref[1:2, :] += jnp.sum(dv * _shift_down(hv, hp, 1), axis=0, keepdims=True)
        dcw_ref[2:3, :] += jnp.sum(dv * hv, axis=0, keepdims=True)

    return pl.pallas_call(
        body, name=name, grid=(ncol, nrow),
        in_specs=[tile(0), nxt32, tile(0), prev(0), vec(3, 0)],
        out_specs=[tile(0), vec(3, 0), vec(1, 0)],
        out_shape=[jax.ShapeDtypeStruct((t, f), MXU_DTYPE), jax.ShapeDtypeStruct((3, f), F32),
                   jax.ShapeDtypeStruct((1, f), F32)],
        compiler_params=_params(("parallel", "arbitrary")),
    )(dhc, dhc, h, h, cw)


def local_step(x, target, w):
    t, d = x.shape
    heads = d // GLA_DV
    saved = []
    cur, cur_mx = x, x
    for i in range(DEPTH):
        j = i // 2
        tag = f"l{i}"
        if i % 2 == 0:
            qkv = matmul(cur_mx, w["sb_qkv"][j], "nn", MXU_DTYPE, f"{tag}_qkv")
            o_mx, o_f32 = sb_attention_fwd(qkv, f"{tag}_sb_fwd")
            y = matmul(o_mx, w["sb_o"][j], "nn", F32, f"{tag}_sb_out")
            mix = (qkv, o_mx, o_f32)
        else:
            proj = matmul(cur_mx, w["gla_in"][j], "nn", F32, f"{tag}_gla_in")
            og, o_pre, states = gla_fwd(proj, w["gla_a2"][j], w["gla_ba"][j], w["gla_ng"][j], heads, f"{tag}_gla_fwd")
            y = matmul(og, w["gla_o"][j], "nn", F32, f"{tag}_gla_out")
            mix = (proj, og, o_pre, states)
        mid, mid_mx, s_mix = resid_ln_fwd(cur, y, w["ln_mix_g"][i], w["ln_mix_b"][i], f"{tag}_ln_mix")
        hu = matmul(mid_mx, w["up_u"][i], "nn", MXU_DTYPE, f"{tag}_up_u")
        hg = matmul(mid_mx, w["up_g"][i], "nn", MXU_DTYPE, f"{tag}_up_g")
        act = conv_gate_fwd(hu, hg, w["cw_u"][i], w["cw_g"][i], w["cb_u"][i], w["cb_g"][i], f"{tag}_conv_fwd")
        y2 = matmul(act, w["down"][i], "nn", F32, f"{tag}_down")
        nxt, nxt_mx, s_ffn = resid_ln_fwd(mid, y2, w["ln_ffn_g"][i], w["ln_ffn_b"][i], f"{tag}_ln_ffn")
        saved.append((cur_mx, mix, s_mix, mid_mx, hu, hg, act, s_ffn))
        cur, cur_mx = nxt, nxt_mx

    loss_cols, dout = loss_and_grad(cur, target, "loss")

    g = {k: [None] * v.shape[0] for k, v in w.items()}
    for i in reversed(range(DEPTH)):
        j = i // 2
        tag = f"l{i}"
        cur_mx, mix, s_mix, mid_mx, hu, hg, act, s_ffn = saved[i]
        ds, ds_mx, g["ln_ffn_g"][i], g["ln_ffn_b"][i] = ln_bwd(dout, s_ffn, w["ln_ffn_g"][i], f"{tag}_ln_ffn_bwd")
        g["down"][i] = matmul(act, ds_mx, "tn", F32, f"{tag}_d_down")
        dact = matmul(ds_mx, w["down"][i], "nt", F32, f"{tag}_dact")
        dhu, dhg = conv_gate_bwd_act(dact, hu, hg, w["cw_u"][i], w["cw_g"][i], w["cb_u"][i], w["cb_g"][i],
                                     f"{tag}_gate_bwd")
        dh_u, g["cw_u"][i], g["cb_u"][i] = conv_bwd(dhu, hu, w["cw_u"][i], f"{tag}_conv_bwd_u")
        dh_g, g["cw_g"][i], g["cb_g"][i] = conv_bwd(dhg, hg, w["cw_g"][i], f"{tag}_conv_bwd_g")
        g["up_u"][i] = matmul(mid_mx, dh_u, "tn", F32, f"{tag}_d_up_u")
        g["up_g"][i] = matmul(mid_mx, dh_g, "tn", F32, f"{tag}_d_up_g")
        dmid = matmul(dh_u, w["up_u"][i], "nt", F32, f"{tag}_dmid_u", resid=ds, resid_scale=ALPHA)
        dmid = matmul(dh_g, w["up_g"][i], "nt", F32, f"{tag}_dmid_g", resid=dmid)
        ds2, ds2_mx, g["ln_mix_g"][i], g["ln_mix_b"][i] = ln_bwd(dmid, s_mix, w["ln_mix_g"][i], f"{tag}_ln_mix_bwd")
        if i % 2 == 0:
            qkv, o_mx, o_f32 = mix
            g["sb_o"][j] = matmul(o_mx, ds2_mx, "tn", F32, f"{tag}_d_sb_o")
            do = matmul(ds2_mx, w["sb_o"][j], "nt", MXU_DTYPE, f"{tag}_do")
            dq, dk, dv = sb_attention_bwd(qkv, do, o_f32, f"{tag}_sb_bwd")
            dqkv = jnp.concatenate([dq, dk, dv], axis=1)
            g["sb_qkv"][j] = matmul(cur_mx, dqkv, "tn", F32, f"{tag}_d_qkv")
            dout = matmul(dqkv, w["sb_qkv"][j], "nt", F32, f"{tag}_dx", resid=ds2, resid_scale=ALPHA)
        else:
            proj, og, o_pre, states = mix
            g["gla_o"][j] = matmul(og, ds2_mx, "tn", F32, f"{tag}_d_gla_o")
            dog = matmul(ds2_mx, w["gla_o"][j], "nt", F32, f"{tag}_dog")
            dproj, g["gla_a2"][j], g["gla_ba"][j], g["gla_ng"][j] = gla_bwd(
                proj, w["gla_a2"][j], w["gla_ba"][j], w["gla_ng"][j], o_pre, states, dog, heads, f"{tag}_gla_bwd")
            g["gla_in"][j] = matmul(cur_mx, dproj, "tn", F32, f"{tag}_d_gla_in")
            dout = matmul(dproj, w["gla_in"][j], "nt", F32, f"{tag}_dx", resid=ds2, resid_scale=ALPHA)
    grads = {k: jnp.stack(v) for k, v in g.items()}
    return loss_cols, dout, grads


MESH_IDS = pl.DeviceIdType.MESH
ANY = pl.BlockSpec(memory_space=pl.ANY)


def _place():
    return lax.axis_index("x"), lax.axis_index("y"), lax.axis_index("c")


def all_gather(shard, name):
    r, c = shard.shape

    def body(x_ref, out_ref, send_sems, recv_sems, local_sem):
        x, y, cc = _place()
        me, sibling = (x, y, cc), (x, y, 1 - cc)
        chips = [(1 - x, y), (x, 1 - y), (1 - x, 1 - y)]

        def slot(px, py, pc):
            return out_ref.at[4 * px + 2 * py + pc]

        def copy(k, block, to, src=None):
            return pltpu.make_async_remote_copy(
                src_ref=slot(*block) if src is None else src, dst_ref=slot(*block),
                send_sem=send_sems.at[k], recv_sem=recv_sems.at[k], device_id=to, device_id_type=MESH_IDS)

        mine = pltpu.make_async_copy(x_ref, slot(*me), local_sem)
        mine.start()
        first = [copy(0, me, sibling, src=x_ref)]
        first += [copy(1 + j, me, (*chip, cc), src=x_ref) for j, chip in enumerate(chips)]
        for cp in first:
            cp.start()
        passed = [copy(4 + j, (*chip, cc), sibling) for j, chip in enumerate(chips)]
        for j, chip in enumerate(chips):
            copy(1 + j, (*chip, cc), me).wait_recv()
            passed[j].start()
        copy(0, sibling, me).wait_recv()
        for j, chip in enumerate(chips):
            copy(4 + j, (*chip, 1 - cc), me).wait_recv()
        for cp in first + passed:
            cp.wait_send()
        mine.wait()

    return pl.pallas_call(
        body, name=name, in_specs=[ANY], out_specs=ANY,
        out_shape=jax.ShapeDtypeStruct((N_DEV, r, c), shard.dtype),
        scratch_shapes=[pltpu.SemaphoreType.DMA((7,)), pltpu.SemaphoreType.DMA((7,)), pltpu.SemaphoreType.DMA],
    )(shard)


def exchange_partials(parts, name):
    _, r, c = parts.shape

    def body(p_ref, out_ref, send_sems, recv_sems, local_sem):
        x, y, cc = _place()
        my_slot = 4 * x + 2 * y + cc

        def peer(m):
            return (x ^ ((m >> 2) & 1), y ^ ((m >> 1) & 1), cc ^ (m & 1))

        def copy(m):
            px, py, pc = peer(m)
            return pltpu.make_async_remote_copy(
                src_ref=p_ref.at[4 * px + 2 * py + pc], dst_ref=out_ref.at[my_slot],
                send_sem=send_sems.at[m - 1], recv_sem=recv_sems.at[m - 1],
                device_id=(px, py, pc), device_id_type=MESH_IDS)

        def arrival(m):
            px, py, pc = peer(m)
            return pltpu.make_async_remote_copy(
                src_ref=p_ref.at[my_slot], dst_ref=out_ref.at[4 * px + 2 * py + pc],
                send_sem=send_sems.at[m - 1], recv_sem=recv_sems.at[m - 1],
                device_id=(px, py, pc), device_id_type=MESH_IDS)

        mine = pltpu.make_async_copy(p_ref.at[my_slot], out_ref.at[my_slot], local_sem)
        mine.start()
        sends = [copy(m) for m in range(1, N_DEV)]
        for cp in sends:
            cp.start()
        for m in range(1, N_DEV):
            arrival(m).wait_recv()
        for cp in sends:
            cp.wait_send()
        mine.wait()

    return pl.pallas_call(
        body, name=name, in_specs=[ANY], out_specs=ANY,
        out_shape=jax.ShapeDtypeStruct(parts.shape, parts.dtype),
        scratch_shapes=[pltpu.SemaphoreType.DMA((7,)), pltpu.SemaphoreType.DMA((7,)), pltpu.SemaphoreType.DMA],
    )(parts)


def sum_slots(parts, name):
    n, r, c = parts.shape
    tr = _pick(r, (256, 128, 64, 32, 16, 8))

    def body(p_ref, o_ref):
        acc = p_ref[0]
        for s in range(1, n):
            acc = acc + p_ref[s]
        o_ref[...] = acc

    return pl.pallas_call(
        body, name=name, grid=(r // tr,),
        in_specs=[pl.BlockSpec((n, tr, c), lambda i: (0, i, 0))],
        out_specs=pl.BlockSpec((tr, c), lambda i: (i, 0)),
        out_shape=jax.ShapeDtypeStruct((r, c), parts.dtype),
        compiler_params=_params(("parallel",)),
    )(parts)


def adamw(w, g, m, v, name):
    shape = w.shape
    cols = shape[-1]
    rows = math.prod(shape[:-1])
    tr = rows if rows <= 512 else _pick(rows, (512, 256, 128, 64, 32, 16, 8))
    two_d = lambda a: a.reshape(rows, cols)

    def body(w_ref, g_ref, m_ref, v_ref, d_ref, nm_ref, nv_ref):
        gv = g_ref[...]
        nm = ADAM_B1 * m_ref[...] + (1.0 - ADAM_B1) * gv
        nv = ADAM_B2 * v_ref[...] + (1.0 - ADAM_B2) * (gv * gv)
        m_hat = nm / (1.0 - ADAM_B1 ** ADAM_STEP)
        v_hat = nv / (1.0 - ADAM_B2 ** ADAM_STEP)
        d_ref[...] = -ADAM_LR * (m_hat / (jnp.sqrt(v_hat) + ADAM_EPS) + ADAM_WD * w_ref[...])
        nm_ref[...] = nm
        nv_ref[...] = nv

    spec = pl.BlockSpec((tr, cols), lambda i: (i, 0))
    out = pl.pallas_call(
        body, name=name, grid=(rows // tr,),
        in_specs=[spec] * 4, out_specs=[spec] * 3,
        out_shape=[jax.ShapeDtypeStruct((rows, cols), F32)] * 3,
        compiler_params=_params(("parallel",)),
    )(two_d(w), two_d(g), two_d(m), two_d(v))
    return tuple(o.reshape(shape) for o in out)


WEIGHT_NAMES = ("sb_w_qkv", "sb_w_o", "gla_w_in", "gla_w_a1", "gla_w_a2", "gla_b_a", "gla_norm_g", "gla_w_o",
                "ffn_w_up", "ffn_conv_w", "ffn_conv_b", "ffn_w_down", "ln_mix_g", "ln_mix_b", "ln_ffn_g", "ln_ffn_b")
SHARD_AXIS = {"sb_w_qkv": 2, "sb_w_o": 1, "gla_w_in": 2, "gla_w_a1": 1, "gla_w_a2": 2, "gla_b_a": 1,
              "gla_norm_g": 1, "gla_w_o": 1, "ffn_w_up": 2, "ffn_conv_w": 2, "ffn_w_down": 1}
MATMUL_WEIGHTS = ("sb_w_qkv", "sb_w_o", "gla_w_in", "gla_w_o", "ffn_w_up", "ffn_w_down")
SMALL_WEIGHTS = ("gla_w_a1", "gla_w_a2", "gla_b_a", "gla_norm_g", "ffn_conv_w")
REPLICATED = ("ffn_conv_b", "ln_mix_g", "ln_mix_b", "ln_ffn_g", "ln_ffn_b")


def _rows_of(n, row_multiple):
    rows = -(-n // PACK_COLS)
    return -(-rows // row_multiple) * row_multiple


def _pack(flat_parts, lead, row_multiple):
    out = []
    for p in flat_parts:
        n = p.shape[-1]
        rows = _rows_of(n, row_multiple)
        pad = rows * PACK_COLS - n
        if pad:
            p = jnp.pad(p, [(0, 0)] * len(lead) + [(0, pad)])
        out.append(p.reshape(*lead, rows, PACK_COLS))
    return jnp.concatenate(out, axis=len(lead))


def _unpack(buf, sizes, row_multiple):
    lead = buf.shape[:-2]
    out, row = [], 0
    for n in sizes:
        r = _rows_of(n, row_multiple)
        piece = lax.slice_in_dim(buf, row, row + r, axis=len(lead)).reshape(*lead, r * PACK_COLS)
        out.append(piece[..., :n])
        row += r
    return out


def _whole_from_shards(stacked, shard_shape, axis):
    a = jnp.moveaxis(stacked.reshape((N_DEV,) + tuple(shard_shape)), 0, axis)
    shape = list(shard_shape)
    shape[axis] *= N_DEV
    return a.reshape(shape)


def _shards_from_whole(whole, axis):
    shape = list(whole.shape)
    a = whole.reshape(shape[:axis] + [N_DEV, shape[axis] // N_DEV] + shape[axis + 1:])
    return jnp.moveaxis(a, axis, 0).reshape(N_DEV, -1)


def kernel(x, sb_w_qkv, sb_w_o, gla_w_in, gla_w_a1, gla_w_a2, gla_b_a, gla_norm_g, gla_w_o, ffn_w_up, ffn_conv_w, ffn_conv_b, ffn_w_down, ln_mix_g, ln_mix_b, ln_ffn_g, ln_ffn_b, loss_target, m_sb_w_qkv, m_sb_w_o, m_gla_w_in, m_gla_w_a1, m_gla_w_a2, m_gla_b_a, m_gla_norm_g, m_gla_w_o, m_ffn_w_up, m_ffn_conv_w, m_ffn_conv_b, m_ffn_w_down, m_ln_mix_g, m_ln_mix_b, m_ln_ffn_g, m_ln_ffn_b, v_sb_w_qkv, v_sb_w_o, v_gla_w_in, v_gla_w_a1, v_gla_w_a2, v_gla_b_a, v_gla_norm_g, v_gla_w_o, v_ffn_w_up, v_ffn_conv_w, v_ffn_conv_b, v_ffn_w_down, v_ln_mix_g, v_ln_mix_b, v_ln_ffn_g, v_ln_ffn_b):
    wts = dict(zip(WEIGHT_NAMES, (sb_w_qkv, sb_w_o, gla_w_in, gla_w_a1, gla_w_a2, gla_b_a, gla_norm_g, gla_w_o,
                                  ffn_w_up, ffn_conv_w, ffn_conv_b, ffn_w_down, ln_mix_g, ln_mix_b, ln_ffn_g, ln_ffn_b)))
    mom1 = dict(zip(WEIGHT_NAMES, (m_sb_w_qkv, m_sb_w_o, m_gla_w_in, m_gla_w_a1, m_gla_w_a2, m_gla_b_a, m_gla_norm_g,
                                   m_gla_w_o, m_ffn_w_up, m_ffn_conv_w, m_ffn_conv_b, m_ffn_w_down, m_ln_mix_g,
                                   m_ln_mix_b, m_ln_ffn_g, m_ln_ffn_b)))
    mom2 = dict(zip(WEIGHT_NAMES, (v_sb_w_qkv, v_sb_w_o, v_gla_w_in, v_gla_w_a1, v_gla_w_a2, v_gla_b_a, v_gla_norm_g,
                                   v_gla_w_o, v_ffn_w_up, v_ffn_conv_w, v_ffn_conv_b, v_ffn_w_down, v_ln_mix_g,
                                   v_ln_mix_b, v_ln_ffn_g, v_ln_ffn_b)))
    t, d = x.shape[1], x.shape[2]
    f = ffn_w_up.shape[2] * N_DEV // 2

    big = all_gather(_pack([wts[n].reshape(-1).astype(MXU_DTYPE) for n in MATMUL_WEIGHTS], (), BF16_ROWS), "gather_matmul_weights")
    small = all_gather(_pack([wts[n].reshape(-1) for n in SMALL_WEIGHTS], (), SUBLANES), "gather_small_weights")
    whole = {}
    for names, buf, mult in ((MATMUL_WEIGHTS, big, BF16_ROWS), (SMALL_WEIGHTS, small, SUBLANES)):
        for n, piece in zip(names, _unpack(buf, [wts[n].size for n in names], mult)):
            whole[n] = _whole_from_shards(piece, wts[n].shape, SHARD_AXIS[n])
    a1 = jnp.pad(whole["gla_w_a1"].astype(MXU_DTYPE), ((0, 0), (0, 0), (0, LANES - GLA_GATE_RANK)))
    w = {
        "sb_qkv": whole["sb_w_qkv"], "sb_o": whole["sb_w_o"],
        "gla_in": jnp.concatenate([whole["gla_w_in"], a1], axis=2), "gla_o": whole["gla_w_o"],
        "up_u": whole["ffn_w_up"][:, :, :f], "up_g": whole["ffn_w_up"][:, :, f:], "down": whole["ffn_w_down"],
        "gla_a2": jnp.pad(whole["gla_w_a2"], ((0, 0), (0, LANES - GLA_GATE_RANK), (0, 0))),
        "gla_ba": whole["gla_b_a"][:, None, :], "gla_ng": whole["gla_norm_g"][:, None, :],
        "cw_u": whole["ffn_conv_w"][:, :, :f], "cw_g": whole["ffn_conv_w"][:, :, f:],
        "cb_u": ffn_conv_b[:, None, :f], "cb_g": ffn_conv_b[:, None, f:],
        "ln_mix_g": ln_mix_g, "ln_mix_b": ln_mix_b, "ln_ffn_g": ln_ffn_g, "ln_ffn_b": ln_ffn_b,
    }

    loss_cols, grad_x, g = local_step(x[0], loss_target[0], w)
    loss = lax.psum(jnp.sum(loss_cols), ("x", "y", "c"))

    full = {
        "sb_w_qkv": g["sb_qkv"], "sb_w_o": g["sb_o"], "gla_w_in": g["gla_in"][:, :, :3 * d],
        "gla_w_a1": g["gla_in"][:, :, 3 * d:3 * d + GLA_GATE_RANK], "gla_w_a2": g["gla_a2"][:, :GLA_GATE_RANK, :],
        "gla_b_a": g["gla_ba"][:, 0], "gla_norm_g": g["gla_ng"][:, 0], "gla_w_o": g["gla_o"],
        "ffn_w_up": jnp.concatenate([g["up_u"], g["up_g"]], axis=2),
        "ffn_conv_w": jnp.concatenate([g["cw_u"], g["cw_g"]], axis=2),
        "ffn_w_down": g["down"],
    }
    sharded = MATMUL_WEIGHTS + SMALL_WEIGHTS
    parts = _pack([_shards_from_whole(full[n], SHARD_AXIS[n]) for n in sharded], (N_DEV,), SUBLANES)
    summed = sum_slots(exchange_partials(parts, "exchange_weight_grads"), "sum_weight_grads")
    grads = {n: piece.reshape(wts[n].shape) for n, piece in zip(sharded, _unpack(summed, [wts[n].size for n in sharded], SUBLANES))}

    rep = {
        "ffn_conv_b": jnp.concatenate([g["cb_u"], g["cb_g"]], axis=2)[:, 0],
        "ln_mix_g": g["ln_mix_g"][:, 0], "ln_mix_b": g["ln_mix_b"][:, 0],
        "ln_ffn_g": g["ln_ffn_g"][:, 0], "ln_ffn_b": g["ln_ffn_b"][:, 0],
    }
    rep_all = all_gather(_pack([rep[n].reshape(-1) for n in REPLICATED], (), SUBLANES), "gather_replicated_grads")
    rep_sum = sum_slots(rep_all, "sum_replicated_grads")
    for n, piece in zip(REPLICATED, _unpack(rep_sum, [wts[n].size for n in REPLICATED], SUBLANES)):
        grads[n] = piece.reshape(wts[n].shape)

    upd = {n: adamw(wts[n], grads[n], mom1[n], mom2[n], f"adamw_{n}") for n in WEIGHT_NAMES}
    return (loss, grad_x[None], *[grads[n] for n in WEIGHT_NAMES], *[upd[n][0] for n in WEIGHT_NAMES],
            *[upd[n][1] for n in WEIGHT_NAMES], *[upd[n][2] for n in WEIGHT_NAMES])
```

```python
import functools
import math

import jax
import jax.numpy as jnp
from jax import lax
from jax.experimental import pallas as pl
from jax.experimental.pallas import tpu as pltpu

F32 = jnp.float32
BF16 = jnp.bfloat16
MXU_DTYPE = jnp.bfloat16

DEPTH = 4
ALPHA = (2.0 * DEPTH) ** 0.25
SB_HEAD_DIM = 64
GLA_DK = 128
GLA_DV = 256
GLA_GATE_RANK = 16
GLA_GATE_TAU = 16.0
GLA_CHUNK = 64
LN_EPS = 1e-5
ADAM_LR = 0.001
ADAM_B1 = 0.9
ADAM_B2 = 0.999
ADAM_EPS = 1e-08
ADAM_WD = 0.01
ADAM_STEP = 10

LANES = 128
SUBLANES = 8
BF16_ROWS = 16
VMEM_LIMIT = 56 * 1024 * 1024
N_DEV = 8
PACK_COLS = 1024

NN = ((1,), (0,))
NT = ((1,), (1,))
TN = ((0,), (0,))


def _dot(a, b, dims):
    return lax.dot_general(a.astype(MXU_DTYPE), b.astype(MXU_DTYPE), (dims, ((), ())),
                           preferred_element_type=F32)


def _split_dot(x, u, dims):
    hi = x.astype(MXU_DTYPE)
    lo = (x - hi.astype(F32)).astype(MXU_DTYPE)
    return (lax.dot_general(hi, u, (dims, ((), ())), preferred_element_type=F32)
            + lax.dot_general(lo, u, (dims, ((), ())), preferred_element_type=F32))


def _split_dot_left(u, x, dims):
    hi = x.astype(MXU_DTYPE)
    lo = (x - hi.astype(F32)).astype(MXU_DTYPE)
    return (lax.dot_general(u, hi, (dims, ((), ())), preferred_element_type=F32)
            + lax.dot_general(u, lo, (dims, ((), ())), preferred_element_type=F32))


def _sigmoid(x):
    return 1.0 / (1.0 + jnp.exp(-x))


def _log_sigmoid(x):
    return jnp.minimum(x, 0.0) - jnp.log(1.0 + jnp.exp(-jnp.abs(x)))


def _pick(dim, candidates):
    for c in candidates:
        if dim % c == 0:
            return c
    raise ValueError(f"no tile for {dim} in {candidates}")


def _params(sem):
    return pltpu.CompilerParams(dimension_semantics=sem, vmem_limit_bytes=VMEM_LIMIT)


def matmul(a, b, mode, out_dtype, name, resid=None, resid_scale=1.0):
    if mode == "nn":
        (m, k), (k2, n) = a.shape, b.shape
    elif mode == "nt":
        (m, k), (n, k2) = a.shape, b.shape
    else:
        (k, m), (k2, n) = a.shape, b.shape
    assert k == k2, (a.shape, b.shape, mode)
    tm = _pick(m, (1024, 512, 256, 128))
    tn = _pick(n, (1024, 704 * 2, 640, 512, 256, 128))
    tk = _pick(k, (1024, 704 * 2, 640, 512, 256, 128))
    nk = k // tk
    dims = {"nn": NN, "nt": NT, "tn": TN}[mode]

    if mode == "tn":
        a_spec = pl.BlockSpec((tk, tm), lambda i, j, kk: (kk, i))
    else:
        a_spec = pl.BlockSpec((tm, tk), lambda i, j, kk: (i, kk))
    if mode == "nt":
        b_spec = pl.BlockSpec((tn, tk), lambda i, j, kk: (j, kk))
    else:
        b_spec = pl.BlockSpec((tk, tn), lambda i, j, kk: (kk, j))
    o_spec = pl.BlockSpec((tm, tn), lambda i, j, kk: (i, j))
    has_resid = resid is not None

    def body(*refs):
        if has_resid:
            a_ref, b_ref, r_ref, o_ref, acc_ref = refs
        else:
            a_ref, b_ref, o_ref, acc_ref = refs
        kk = pl.program_id(2)

        @pl.when(kk == 0)
        def _():
            acc_ref[...] = jnp.zeros_like(acc_ref)

        acc_ref[...] += _dot(a_ref[...], b_ref[...], dims)

        @pl.when(kk == nk - 1)
        def _():
            acc = acc_ref[...]
            if has_resid:
                acc = acc + resid_scale * r_ref[...]
            o_ref[...] = acc.astype(out_dtype)

    in_specs = [a_spec, b_spec] + ([o_spec] if has_resid else [])
    args = (a, b) + ((resid,) if has_resid else ())
    return pl.pallas_call(
        body, name=name, grid=(m // tm, n // tn, nk),
        in_specs=in_specs, out_specs=o_spec,
        out_shape=jax.ShapeDtypeStruct((m, n), out_dtype),
        scratch_shapes=[pltpu.VMEM((tm, tn), F32)],
        compiler_params=_params(("parallel", "parallel", "arbitrary")),
    )(*args)


def resid_ln_fwd(x, y, g, b, name):
    t, d = x.shape
    tm = _pick(t, (256, 128))

    def body(x_ref, y_ref, g_ref, b_ref, o_ref, ob_ref, s_ref):
        s = ALPHA * x_ref[...] + y_ref[...]
        mu = jnp.mean(s, axis=-1, keepdims=True)
        c = s - mu
        var = jnp.mean(c * c, axis=-1, keepdims=True)
        out = c * lax.rsqrt(var + LN_EPS) * g_ref[...] + b_ref[...]
        s_ref[...] = s
        o_ref[...] = out
        ob_ref[...] = out.astype(MXU_DTYPE)

    row = pl.BlockSpec((tm, d), lambda i: (i, 0))
    vec = pl.BlockSpec((1, d), lambda i: (0, 0))
    return pl.pallas_call(
        body, name=name, grid=(t // tm,),
        in_specs=[row, row, vec, vec], out_specs=[row, row, row],
        out_shape=[jax.ShapeDtypeStruct((t, d), F32), jax.ShapeDtypeStruct((t, d), MXU_DTYPE),
                   jax.ShapeDtypeStruct((t, d), F32)],
        compiler_params=_params(("parallel",)),
    )(x, y, g.reshape(1, d), b.reshape(1, d))


def ln_bwd(dout, s, g, name):
    t, d = s.shape
    tm = _pick(t, (256, 128))

    def body(do_ref, s_ref, g_ref, ds_ref, dsb_ref, dg_ref, db_ref):
        @pl.when(pl.program_id(0) == 0)
        def _():
            dg_ref[...] = jnp.zeros_like(dg_ref)
            db_ref[...] = jnp.zeros_like(db_ref)

        sv = s_ref[...]
        mu = jnp.mean(sv, axis=-1, keepdims=True)
        c = sv - mu
        var = jnp.mean(c * c, axis=-1, keepdims=True)
        rstd = lax.rsqrt(var + LN_EPS)
        xhat = c * rstd
        do = do_ref[...]
        dxh = do * g_ref[...]
        m1 = jnp.mean(dxh, axis=-1, keepdims=True)
        m2 = jnp.mean(dxh * xhat, axis=-1, keepdims=True)
        ds = rstd * (dxh - m1 - xhat * m2)
        ds_ref[...] = ds
        dsb_ref[...] = ds.astype(MXU_DTYPE)
        dg_ref[...] += jnp.sum(do * xhat, axis=0, keepdims=True)
        db_ref[...] += jnp.sum(do, axis=0, keepdims=True)

    row = pl.BlockSpec((tm, d), lambda i: (i, 0))
    vec = pl.BlockSpec((1, d), lambda i: (0, 0))
    return pl.pallas_call(
        body, name=name, grid=(t // tm,),
        in_specs=[row, row, vec], out_specs=[row, row, vec, vec],
        out_shape=[jax.ShapeDtypeStruct((t, d), F32), jax.ShapeDtypeStruct((t, d), MXU_DTYPE),
                   jax.ShapeDtypeStruct((1, d), F32), jax.ShapeDtypeStruct((1, d), F32)],
        compiler_params=_params(("arbitrary",)),
    )(dout, s, g.reshape(1, d))


def loss_and_grad(y, target, name):
    t, d = y.shape
    tm = _pick(t, (256, 128))

    def body(y_ref, t_ref, l_ref, dy_ref):
        @pl.when(pl.program_id(0) == 0)
        def _():
            l_ref[...] = jnp.zeros_like(l_ref)

        e = y_ref[...] - t_ref[...]
        dy_ref[...] = e / d
        l_ref[...] += jnp.sum(e * e, axis=0, keepdims=True) * (0.5 / d)

    row = pl.BlockSpec((tm, d), lambda i: (i, 0))
    vec = pl.BlockSpec((1, d), lambda i: (0, 0))
    return pl.pallas_call(
        body, name=name, grid=(t // tm,),
        in_specs=[row, row], out_specs=[vec, row],
        out_shape=[jax.ShapeDtypeStruct((1, d), F32), jax.ShapeDtypeStruct((t, d), F32)],
        compiler_params=_params(("arbitrary",)),
    )(y, target)


SB_TQ = 1024
SB_TK = 128
SB_UNROLL = 2


def _suffix_matrix(tk, inclusive):
    r = lax.broadcasted_iota(jnp.int32, (tk, tk), 0)
    c = lax.broadcasted_iota(jnp.int32, (tk, tk), 1)
    return ((r >= c) if inclusive else (r > c)).astype(MXU_DTYPE)


def _sb_block(qa, kj, run_l, upper, diag_mask):
    z = _dot(qa, kj, NT)
    sp = jnp.log(1.0 + jnp.exp(-jnp.abs(z)))
    ls_pos = jnp.minimum(z, 0.0) - sp
    ls_neg = -jnp.maximum(z, 0.0) - sp
    if diag_mask is not None:
        ls_neg = jnp.where(diag_mask, ls_neg, 0.0)
    a = jnp.exp(ls_pos + _split_dot(ls_neg, upper, NN) + run_l)
    if diag_mask is not None:
        a = jnp.where(diag_mask, a, 0.0)
    return ls_pos, jnp.sum(ls_neg, axis=-1, keepdims=True), a


def sb_attention_fwd(qkv, name):
    t, d3 = qkv.shape
    d = d3 // 3
    npair = d // LANES
    tq, tk = min(SB_TQ, t), SB_TK
    nq, ratio = t // tq, tq // tk
    scale = SB_HEAD_DIM ** -0.5

    def body(q_ref, k_ref, v_ref, o_ref, of_ref, qs_ref, acc_ref, fine_ref, run_ref):
        i = pl.program_id(1)
        lane = lax.broadcasted_iota(jnp.int32, (1, LANES), 1)
        upper = _suffix_matrix(tk, inclusive=False)
        q2 = q_ref[...]
        for a in range(2):
            qs_ref[a] = jnp.where(lane // SB_HEAD_DIM == a, q2, jnp.zeros_like(q2)) * scale
        acc_ref[...] = jnp.zeros_like(acc_ref)
        fine_ref[...] = jnp.zeros_like(fine_ref)
        run_ref[...] = jnp.zeros_like(run_ref)

        def block(start, row0, mask):
            kj = k_ref[pl.ds(start, tk), :]
            vj = v_ref[pl.ds(start, tk), :]
            for a in range(2):
                _, total, p = _sb_block(qs_ref[a, row0:, :], kj, run_ref[a, row0:, :], upper, mask)
                p_mx = p.astype(MXU_DTYPE)
                acc_ref[a, row0:, :] += _dot(p_mx, vj, NN)
                fine_ref[a, row0:, :] += _dot(p - p_mx.astype(F32), vj, NN)
                run_ref[a, row0:, :] += total

        for back in range(ratio):
            row0 = (ratio - 1 - back) * tk
            rows = tq - row0
            mask = (lax.broadcasted_iota(jnp.int32, (rows, tk), 1)
                    < lax.broadcasted_iota(jnp.int32, (rows, tk), 0))
            block(pl.multiple_of(i * tq + row0, tk), row0, mask)

        def earlier(n, _):
            for u in range(SB_UNROLL):
                block(pl.multiple_of((i * ratio - 1 - SB_UNROLL * n - u) * tk, tk), 0, None)
            return 0

        lax.fori_loop(0, i * ratio // SB_UNROLL, earlier, 0)
        o = jnp.where(lane < SB_HEAD_DIM, acc_ref[0], acc_ref[1])
        of_ref[...] = o + jnp.where(lane < SB_HEAD_DIM, fine_ref[0], fine_ref[1])
        o_ref[...] = o.astype(MXU_DTYPE)

    q_spec = pl.BlockSpec((tq, LANES), lambda p, i: (i, p))
    k_spec = pl.BlockSpec((t, LANES), lambda p, i: (0, npair + p))
    v_spec = pl.BlockSpec((t, LANES), lambda p, i: (0, 2 * npair + p))
    o_spec = pl.BlockSpec((tq, LANES), lambda p, i: (i, p))
    return pl.pallas_call(
        body, name=name, grid=(npair, nq),
        in_specs=[q_spec, k_spec, v_spec], out_specs=[o_spec, o_spec],
        out_shape=[jax.ShapeDtypeStruct((t, d), MXU_DTYPE), jax.ShapeDtypeStruct((t, d), F32)],
        scratch_shapes=[pltpu.VMEM((2, tq, LANES), MXU_DTYPE), pltpu.VMEM((2, tq, LANES), F32),
                        pltpu.VMEM((2, tq, LANES), F32), pltpu.VMEM((2, tq, 1), F32)],
        compiler_params=_params(("parallel", "arbitrary")),
    )(qkv, qkv, qkv)


def sb_attention_bwd(qkv, do, o_f32, name):
    t, d3 = qkv.shape
    d = d3 // 3
    npair = d // LANES
    tq, tk = min(SB_TQ, t), SB_TK
    nq, ratio = t // tq, tq // tk
    scale = SB_HEAD_DIM ** -0.5

    def body(q_ref, k_ref, v_ref, do_ref, of_ref, dq_ref, dk_ref, dv_ref,
             dk_acc, dv_acc, qs_ref, dos_ref, qst_ref, dost_ref, dq_acc, etot_ref, runl_ref, rune_ref):
        i = pl.program_id(1)

        @pl.when(i == 0)
        def _():
            dk_acc[...] = jnp.zeros_like(dk_acc)
            dv_acc[...] = jnp.zeros_like(dv_acc)

        lane = lax.broadcasted_iota(jnp.int32, (1, LANES), 1)
        upper = _suffix_matrix(tk, inclusive=False)
        upper_incl = _suffix_matrix(tk, inclusive=True)
        q2 = q_ref[...].astype(F32)
        do2 = do_ref[...].astype(F32)
        prod = do2 * of_ref[...]
        for a in range(2):
            head = lane // SB_HEAD_DIM == a
            qa = jnp.where(head, q2, 0.0) * scale
            doa = jnp.where(head, do2, 0.0)
            qs_ref[a] = qa.astype(MXU_DTYPE)
            dos_ref[a] = doa.astype(MXU_DTYPE)
            qst_ref[a] = jnp.transpose(qa).astype(MXU_DTYPE)
            dost_ref[a] = jnp.transpose(doa).astype(MXU_DTYPE)
            etot_ref[a] = jnp.sum(jnp.where(head, prod, 0.0), axis=-1, keepdims=True)
        dq_acc[...] = jnp.zeros_like(dq_acc)
        runl_ref[...] = jnp.zeros_like(runl_ref)
        rune_ref[...] = jnp.zeros_like(rune_ref)

        def block(j, row0, mask):
            start = pl.multiple_of(j * tk, tk)
            kj = k_ref[pl.ds(start, tk), :]
            vj = v_ref[pl.ds(start, tk), :]
            dk_blk = jnp.zeros((LANES, tk), F32)
            dv_blk = jnp.zeros((LANES, tk), F32)
            for a in range(2):
                qa = qs_ref[a, row0:, :]
                doa = dos_ref[a, row0:, :]
                ls_pos, total_l, p = _sb_block(qa, kj, runl_ref[a, row0:, :], upper, mask)
                beta = jnp.exp(ls_pos)
                e = p * _dot(doa, vj, NT)
                q_incl = _split_dot(e, upper_incl, NN) + rune_ref[a, row0:, :]
                dz = e * (1.0 - beta) - beta * (etot_ref[a, row0:, :] - q_incl)
                if mask is not None:
                    dz = jnp.where(mask, dz, 0.0)
                dz = dz.astype(MXU_DTYPE)
                dq_acc[a, row0:, :] += _dot(dz, kj, NN)
                dk_blk = dk_blk + _dot(qst_ref[a, :, row0:], dz, NN)
                dv_blk = dv_blk + _dot(dost_ref[a, :, row0:], p, NN)
                runl_ref[a, row0:, :] += total_l
                rune_ref[a, row0:, :] += jnp.sum(e, axis=-1, keepdims=True)
            dk_acc[j] += dk_blk
            dv_acc[j] += dv_blk

        for back in range(ratio):
            row0 = (ratio - 1 - back) * tk
            rows = tq - row0
            mask = (lax.broadcasted_iota(jnp.int32, (rows, tk), 1)
                    < lax.broadcasted_iota(jnp.int32, (rows, tk), 0))
            block(i * ratio + ratio - 1 - back, row0, mask)

        def earlier(n, _):
            for u in range(SB_UNROLL):
                block(i * ratio - 1 - SB_UNROLL * n - u, 0, None)
            return 0

        lax.fori_loop(0, i * ratio // SB_UNROLL, earlier, 0)
        dq = jnp.where(lane < SB_HEAD_DIM, dq_acc[0], dq_acc[1]) * scale
        dq_ref[...] = dq.astype(MXU_DTYPE)

        @pl.when(i == nq - 1)
        def _():
            def untranspose(j, _):
                rows = pl.ds(pl.multiple_of(j * tk, tk), tk)
                dk_ref[rows, :] = jnp.transpose(dk_acc[j]).astype(MXU_DTYPE)
                dv_ref[rows, :] = jnp.transpose(dv_acc[j]).astype(MXU_DTYPE)
                return 0

            lax.fori_loop(0, t // tk, untranspose, 0)

    q_spec = pl.BlockSpec((tq, LANES), lambda p, i: (i, p))
    k_spec = pl.BlockSpec((t, LANES), lambda p, i: (0, npair + p))
    v_spec = pl.BlockSpec((t, LANES), lambda p, i: (0, 2 * npair + p))
    full = pl.BlockSpec((t, LANES), lambda p, i: (0, p))
    return pl.pallas_call(
        body, name=name, grid=(npair, nq),
        in_specs=[q_spec, k_spec, v_spec, q_spec, q_spec], out_specs=[q_spec, full, full],
        out_shape=[jax.ShapeDtypeStruct((t, d), MXU_DTYPE)] * 3,
        scratch_shapes=[pltpu.VMEM((t // tk, LANES, tk), F32), pltpu.VMEM((t // tk, LANES, tk), F32),
                        pltpu.VMEM((2, tq, LANES), MXU_DTYPE), pltpu.VMEM((2, tq, LANES), MXU_DTYPE),
                        pltpu.VMEM((2, LANES, tq), MXU_DTYPE), pltpu.VMEM((2, LANES, tq), MXU_DTYPE),
                        pltpu.VMEM((2, tq, LANES), F32), pltpu.VMEM((2, tq, 1), F32),
                        pltpu.VMEM((2, tq, 1), F32), pltpu.VMEM((2, tq, 1), F32)],
        compiler_params=_params(("parallel", "arbitrary")),
    )(qkv, qkv, qkv, do, o_f32)


GLA_ROWS = 256


def _gla_cols(heads):
    kd, vd = heads * GLA_DK, heads * GLA_DV
    return kd, vd, 2 * kd, 2 * kd + vd, 2 * kd + 2 * vd


def _gla_chunk(proj_ref, rows, h, heads, wa2, ba):
    kd, vd, v0, r0, g0 = _gla_cols(heads)
    c = GLA_CHUNK
    q = proj_ref[rows, h * GLA_DK:(h + 1) * GLA_DK]
    k = proj_ref[rows, kd + h * GLA_DK:kd + (h + 1) * GLA_DK]
    v = proj_ref[rows, v0 + h * GLA_DV:v0 + (h + 1) * GLA_DV]
    xa = proj_ref[rows, g0:g0 + LANES]
    gp = _dot(xa, wa2, NN) + ba
    g = _log_sigmoid(gp) / GLA_GATE_TAU
    r_i = lax.broadcasted_iota(jnp.int32, (c, c), 0)
    c_i = lax.broadcasted_iota(jnp.int32, (c, c), 1)
    causal = c_i <= r_i
    b = _split_dot_left(causal.astype(MXU_DTYPE), g, NN)
    bl = b[c - 1:c, :]
    qd = q * (GLA_DK ** -0.5) * jnp.exp(b)
    ki = k * jnp.exp(-b)
    kdec = k * jnp.exp(bl - b)
    attn = jnp.where(causal, _dot(qd, ki, NT), 0.0)
    return q, k, v, xa, gp, b, bl, qd, ki, kdec, attn, causal


def gla_fwd(proj, wa2, ba, norm_g, heads, name):
    t, width = proj.shape
    kd, vd, v0, r0, g0 = _gla_cols(heads)
    assert width == g0 + LANES
    c = GLA_CHUNK
    rg = min(GLA_ROWS, t)
    ncs = rg // c

    def body(proj_ref, wa2_ref, ba_ref, ng_ref, og_ref, o_ref, st_ref, state):
        @pl.when(pl.program_id(0) == 0)
        def _():
            state[...] = jnp.zeros_like(state)

        ng = ng_ref[...]
        for h in range(heads):
            wa2_h = wa2_ref[:, h * GLA_DK:(h + 1) * GLA_DK]
            ba_h = ba_ref[:, h * GLA_DK:(h + 1) * GLA_DK]

            def chunk(n, _):
                rows = pl.ds(pl.multiple_of(n * c, c), c)
                q, k, v, xa, gp, b, bl, qd, ki, kdec, attn, _ = _gla_chunk(proj_ref, rows, h, heads, wa2_h, ba_h)
                s_t = state[h]
                st_ref[n, h] = s_t
                o = _dot(attn, v, NN) + _dot(qd, s_t, NT)
                state[h] = s_t * jnp.exp(bl) + _dot(v, kdec, TN)
                o_ref[rows, h * GLA_DV:(h + 1) * GLA_DV] = o
                rstd = lax.rsqrt(jnp.mean(o * o, axis=-1, keepdims=True) + LN_EPS)
                r = proj_ref[rows, r0 + h * GLA_DV:r0 + (h + 1) * GLA_DV]
                og = o * rstd * ng * (r * _sigmoid(r))
                og_ref[rows, h * GLA_DV:(h + 1) * GLA_DV] = og.astype(MXU_DTYPE)
                return 0

            lax.fori_loop(0, ncs, chunk, 0)

    return pl.pallas_call(
        body, name=name, grid=(t // rg,),
        in_specs=[pl.BlockSpec((rg, width), lambda i: (i, 0)),
                  pl.BlockSpec((LANES, kd), lambda i: (0, 0)),
                  pl.BlockSpec((1, kd), lambda i: (0, 0)),
                  pl.BlockSpec((1, GLA_DV), lambda i: (0, 0))],
        out_specs=[pl.BlockSpec((rg, vd), lambda i: (i, 0)),
                   pl.BlockSpec((rg, vd), lambda i: (i, 0)),
                   pl.BlockSpec((ncs, heads, GLA_DV, GLA_DK), lambda i: (i, 0, 0, 0))],
        out_shape=[jax.ShapeDtypeStruct((t, vd), MXU_DTYPE), jax.ShapeDtypeStruct((t, vd), F32),
                   jax.ShapeDtypeStruct((t // c, heads, GLA_DV, GLA_DK), F32)],
        scratch_shapes=[pltpu.VMEM((heads, GLA_DV, GLA_DK), F32)],
        compiler_params=_params(("arbitrary",)),
    )(proj, wa2, ba, norm_g)


def gla_bwd(proj, wa2, ba, norm_g, o_pre, states, dog, heads, name):
    t, width = proj.shape
    kd, vd, v0, r0, g0 = _gla_cols(heads)
    c = GLA_CHUNK
    rg = min(GLA_ROWS, t)
    ncs = rg // c
    ngrp = t // rg

    def body(proj_ref, wa2_ref, ba_ref, ng_ref, o_ref, st_ref, dog_ref,
             dproj_ref, dwa2_ref, dba_ref, dng_ref, dstate):
        @pl.when(pl.program_id(0) == 0)
        def _():
            dstate[...] = jnp.zeros_like(dstate)
            dwa2_ref[...] = jnp.zeros_like(dwa2_ref)
            dba_ref[...] = jnp.zeros_like(dba_ref)
            dng_ref[...] = jnp.zeros_like(dng_ref)

        ng = ng_ref[...]
        dproj_ref[:, g0:g0 + LANES] = jnp.zeros((rg, LANES), F32)
        for h in range(heads):
            wa2_h = wa2_ref[:, h * GLA_DK:(h + 1) * GLA_DK]
            ba_h = ba_ref[:, h * GLA_DK:(h + 1) * GLA_DK]

            def chunk(m, _):
                n = ncs - 1 - m
                rows = pl.ds(pl.multiple_of(n * c, c), c)
                q, k, v, xa, gp, b, bl, qd, ki, kdec, attn, causal = _gla_chunk(
                    proj_ref, rows, h, heads, wa2_h, ba_h)
                s_t = st_ref[n, h]
                ds_t = dstate[h]
                o = o_ref[rows, h * GLA_DV:(h + 1) * GLA_DV]
                r = proj_ref[rows, r0 + h * GLA_DV:r0 + (h + 1) * GLA_DV]
                dg_out = dog_ref[rows, h * GLA_DV:(h + 1) * GLA_DV]
                rstd = lax.rsqrt(jnp.mean(o * o, axis=-1, keepdims=True) + LN_EPS)
                ohat = o * rstd
                sig = _sigmoid(r)
                silu = r * sig
                dn = dg_out * silu
                dr = dg_out * (ohat * ng) * (sig * (1.0 + r * (1.0 - sig)))
                dng_ref[...] += jnp.sum(dn * ohat, axis=0, keepdims=True)
                dohat = dn * ng
                d_o = rstd * (dohat - ohat * jnp.mean(dohat * ohat, axis=-1, keepdims=True))
                d_attn = jnp.where(causal, _dot(d_o, v, NT), 0.0)
                dv = _dot(attn, d_o, TN) + _dot(kdec, ds_t, NT)
                dqd = _dot(d_attn, ki, NN) + _dot(d_o, s_t, NN)
                dki = _dot(d_attn, qd, TN)
                dkdec = _dot(v, ds_t, NN)
                decay = jnp.exp(bl)
                dbl = jnp.sum(ds_t * s_t, axis=0, keepdims=True) * decay
                dstate[h] = ds_t * decay + _dot(d_o, qd, TN)
                dq = dqd * (GLA_DK ** -0.5) * jnp.exp(b)
                dk = dki * jnp.exp(-b) + dkdec * jnp.exp(bl - b)
                db = dqd * qd - dki * ki - dkdec * kdec
                dbl = dbl + jnp.sum(dkdec * kdec, axis=0, keepdims=True)
                last = lax.broadcasted_iota(jnp.int32, (c, 1), 0) == c - 1
                db = db + jnp.where(last, dbl, 0.0)
                r_i = lax.broadcasted_iota(jnp.int32, (c, c), 0)
                c_i = lax.broadcasted_iota(jnp.int32, (c, c), 1)
                dg = _split_dot_left((c_i >= r_i).astype(MXU_DTYPE), db, NN)
                dgp = dg * (1.0 / GLA_GATE_TAU) * _sigmoid(-gp)
                dproj_ref[rows, h * GLA_DK:(h + 1) * GLA_DK] = dq
                dproj_ref[rows, kd + h * GLA_DK:kd + (h + 1) * GLA_DK] = dk
                dproj_ref[rows, v0 + h * GLA_DV:v0 + (h + 1) * GLA_DV] = dv
                dproj_ref[rows, r0 + h * GLA_DV:r0 + (h + 1) * GLA_DV] = dr
                dproj_ref[rows, g0:g0 + LANES] += _dot(dgp, wa2_h, NT)
                dwa2_ref[:, h * GLA_DK:(h + 1) * GLA_DK] += _dot(xa, dgp, TN)
                dba_ref[:, h * GLA_DK:(h + 1) * GLA_DK] += jnp.sum(dgp, axis=0, keepdims=True)
                return 0

            lax.fori_loop(0, ncs, chunk, 0)

    rev = lambda i: (ngrp - 1 - i, 0)
    return pl.pallas_call(
        body, name=name, grid=(ngrp,),
        in_specs=[pl.BlockSpec((rg, width), rev),
                  pl.BlockSpec((LANES, kd), lambda i: (0, 0)),
                  pl.BlockSpec((1, kd), lambda i: (0, 0)),
                  pl.BlockSpec((1, GLA_DV), lambda i: (0, 0)),
                  pl.BlockSpec((rg, vd), rev),
                  pl.BlockSpec((ncs, heads, GLA_DV, GLA_DK), lambda i: (ngrp - 1 - i, 0, 0, 0)),
                  pl.BlockSpec((rg, vd), rev)],
        out_specs=[pl.BlockSpec((rg, width), rev),
                   pl.BlockSpec((LANES, kd), lambda i: (0, 0)),
                   pl.BlockSpec((1, kd), lambda i: (0, 0)),
                   pl.BlockSpec((1, GLA_DV), lambda i: (0, 0))],
        out_shape=[jax.ShapeDtypeStruct((t, width), F32), jax.ShapeDtypeStruct((LANES, kd), F32),
                   jax.ShapeDtypeStruct((1, kd), F32), jax.ShapeDtypeStruct((1, GLA_DV), F32)],
        scratch_shapes=[pltpu.VMEM((heads, GLA_DV, GLA_DK), F32)],
        compiler_params=_params(("arbitrary",)),
    )(proj, wa2, ba, norm_g, o_pre, states, dog)


CONV_ROWS = 512
CONV_COLS = 256


def _shift_down(xv, halo_rows, nshift):
    rolled = pltpu.roll(xv, nshift, axis=0)
    ridx = lax.broadcasted_iota(jnp.int32, xv.shape, 0)
    out = rolled
    for r in range(nshift):
        src = halo_rows[halo_rows.shape[0] - nshift + r:halo_rows.shape[0] - nshift + r + 1, :]
        out = jnp.where(ridx == r, src, out)
    return out


def _shift_up(xv, halo_rows, nshift):
    n = xv.shape[0]
    rolled = pltpu.roll(xv, n - nshift, axis=0)
    ridx = lax.broadcasted_iota(jnp.int32, xv.shape, 0)
    out = rolled
    for r in range(nshift):
        out = jnp.where(ridx == n - nshift + r, halo_rows[r:r + 1, :], out)
    return out


def _mxu_rounded(v):
    return v.astype(MXU_DTYPE).astype(F32)


def _conv3(hv, prev, cw, cb):
    cw = _mxu_rounded(cw)
    return cw[0:1, :] * _shift_down(hv, prev, 2) + cw[1:2, :] * _shift_down(hv, prev, 1) + cw[2:3, :] * hv + cb


def _conv_specs(t, f):
    tm = _pick(t, (CONV_ROWS, 256, 128))
    tc = _pick(f, (CONV_COLS, 128))
    hb = BF16_ROWS
    nrow, ncol = t // tm, f // tc
    per = tm // hb
    tile = lambda off: pl.BlockSpec((tm, tc), lambda j, i: (i, off + j))
    prev = lambda off: pl.BlockSpec((hb, tc), lambda j, i: (jnp.maximum(i * per - 1, 0), off + j))
    nxt = lambda off: pl.BlockSpec((hb, tc), lambda j, i: (jnp.minimum((i + 1) * per, t // hb - 1), off + j))
    vec = lambda rows, off: pl.BlockSpec((rows, tc), lambda j, i: (0, off + j))
    return tm, tc, nrow, ncol, tile, prev, nxt, vec


def conv_gate_fwd(hu, hg, cwu, cwg, cbu, cbg, name):
    t, f = hu.shape
    tm, tc, nrow, ncol, tile, prev, nxt, vec = _conv_specs(t, f)

    def body(hu_ref, hg_ref, pu_ref, pg_ref, cwu_ref, cwg_ref, cbu_ref, cbg_ref, act_ref):
        first = pl.program_id(1) == 0
        pu = jnp.where(first, 0.0, pu_ref[...].astype(F32))
        pg = jnp.where(first, 0.0, pg_ref[...].astype(F32))
        u = _conv3(hu_ref[...].astype(F32), pu, cwu_ref[...], cbu_ref[...])
        g = _conv3(hg_ref[...].astype(F32), pg, cwg_ref[...], cbg_ref[...])
        act_ref[...] = (g * _sigmoid(g) * u).astype(MXU_DTYPE)

    return pl.pallas_call(
        body, name=name, grid=(ncol, nrow),
        in_specs=[tile(0), tile(0), prev(0), prev(0), vec(3, 0), vec(3, 0), vec(1, 0), vec(1, 0)],
        out_specs=tile(0),
        out_shape=jax.ShapeDtypeStruct((t, f), MXU_DTYPE),
        compiler_params=_params(("parallel", "arbitrary")),
    )(hu, hg, hu, hg, cwu, cwg, cbu, cbg)


def conv_gate_bwd_act(dact, hu, hg, cwu, cwg, cbu, cbg, name):
    t, f = hu.shape
    tm, tc, nrow, ncol, tile, prev, nxt, vec = _conv_specs(t, f)

    def body(da_ref, hu_ref, hg_ref, pu_ref, pg_ref, cwu_ref, cwg_ref, cbu_ref, cbg_ref, du_ref, dg_ref):
        first = pl.program_id(1) == 0
        pu = jnp.where(first, 0.0, pu_ref[...].astype(F32))
        pg = jnp.where(first, 0.0, pg_ref[...].astype(F32))
        u = _conv3(hu_ref[...].astype(F32), pu, cwu_ref[...], cbu_ref[...])
        g = _conv3(hg_ref[...].astype(F32), pg, cwg_ref[...], cbg_ref[...])
        da = da_ref[...].astype(F32)
        sig = _sigmoid(g)
        du_ref[...] = da * (g * sig)
        dg_ref[...] = da * u * (sig * (1.0 + g * (1.0 - sig)))

    out = pl.pallas_call(
        body, name=name, grid=(ncol, nrow),
        in_specs=[tile(0), tile(0), tile(0), prev(0), prev(0),
                  vec(3, 0), vec(3, 0), vec(1, 0), vec(1, 0)],
        out_specs=[tile(0), tile(0)],
        out_shape=[jax.ShapeDtypeStruct((t, f), F32), jax.ShapeDtypeStruct((t, f), F32)],
        compiler_params=_params(("parallel", "arbitrary")),
    )(dact, hu, hg, hu, hg, cwu, cwg, cbu, cbg)
    return out


def conv_bwd(dhc, h, cw, name):
    t, f = dhc.shape
    tm, tc, nrow, ncol, tile, prev, nxt, vec = _conv_specs(t, f)
    hb = BF16_ROWS
    per = tm // hb
    nxt32 = pl.BlockSpec((hb, tc), lambda j, i: (jnp.minimum((i + 1) * per, t // hb - 1), j))

    def body(d_ref, dn_ref, h_ref, hp_ref, cw_ref, dh_ref, dcw_ref, dcb_ref):
        i = pl.program_id(1)

        @pl.when(i == 0)
        def _():
            dcw_ref[...] = jnp.zeros_like(dcw_ref)
            dcb_ref[...] = jnp.zeros_like(dcb_ref)

        dcb_ref[...] += jnp.sum(d_ref[...], axis=0, keepdims=True)
        dv = _mxu_rounded(d_ref[...])
        nx = jnp.where(i == nrow - 1, 0.0, _mxu_rounded(dn_ref[...]))
        hp = jnp.where(i == 0, 0.0, hp_ref[...].astype(F32))
        hv = h_ref[...].astype(F32)
        cwv = _mxu_rounded(cw_ref[...])
        dh = cwv[2:3, :] * dv + cwv[1:2, :] * _shift_up(dv, nx, 1) + cwv[0:1, :] * _shift_up(dv, nx, 2)
        dh_ref[...] = dh.astype(MXU_DTYPE)
        dcw_ref[0:1, :] += jnp.sum(dv * _shift_down(hv, hp, 2), axis=0, keepdims=True)
        dcw_ref[1:2, :] += jnp.sum(dv * _shift_down(hv, hp, 1), axis=0, keepdims=True)
        dcw_ref[2:3, :] += jnp.sum(dv * hv, axis=0, keepdims=True)

    return pl.pallas_call(
        body, name=name, grid=(ncol, nrow),
        in_specs=[tile(0), nxt32, tile(0), prev(0), vec(3, 0)],
        out_specs=[tile(0), vec(3, 0), vec(1, 0)],
        out_shape=[jax.ShapeDtypeStruct((t, f), MXU_DTYPE), jax.ShapeDtypeStruct((3, f), F32),
                   jax.ShapeDtypeStruct((1, f), F32)],
        compiler_params=_params(("parallel", "arbitrary")),
    )(dhc, dhc, h, h, cw)


def local_step(x, target, w):
    t, d = x.shape
    heads = d // GLA_DV
    saved = []
    cur, cur_mx = x, x
    for i in range(DEPTH):
        j = i // 2
        tag = f"l{i}"
        if i % 2 == 0:
            qkv = matmul(cur_mx, w["sb_qkv"][j], "nn", MXU_DTYPE, f"{tag}_qkv")
            o_mx, o_f32 = sb_attention_fwd(qkv, f"{tag}_sb_fwd")
            y = matmul(o_mx, w["sb_o"][j], "nn", F32, f"{tag}_sb_out")
            mix = (qkv, o_mx, o_f32)
        else:
            proj = matmul(cur_mx, w["gla_in"][j], "nn", F32, f"{tag}_gla_in")
            og, o_pre, states = gla_fwd(proj, w["gla_a2"][j], w["gla_ba"][j], w["gla_ng"][j], heads, f"{tag}_gla_fwd")
            y = matmul(og, w["gla_o"][j], "nn", F32, f"{tag}_gla_out")
            mix = (proj, og, o_pre, states)
        mid, mid_mx, s_mix = resid_ln_fwd(cur, y, w["ln_mix_g"][i], w["ln_mix_b"][i], f"{tag}_ln_mix")
        hu = matmul(mid_mx, w["up_u"][i], "nn", MXU_DTYPE, f"{tag}_up_u")
        hg = matmul(mid_mx, w["up_g"][i], "nn", MXU_DTYPE, f"{tag}_up_g")
        act = conv_gate_fwd(hu, hg, w["cw_u"][i], w["cw_g"][i], w["cb_u"][i], w["cb_g"][i], f"{tag}_conv_fwd")
        y2 = matmul(act, w["down"][i], "nn", F32, f"{tag}_down")
        nxt, nxt_mx, s_ffn = resid_ln_fwd(mid, y2, w["ln_ffn_g"][i], w["ln_ffn_b"][i], f"{tag}_ln_ffn")
        saved.append((cur_mx, mix, s_mix, mid_mx, hu, hg, act, s_ffn))
        cur, cur_mx = nxt, nxt_mx

    loss_cols, dout = loss_and_grad(cur, target, "loss")

    g = {k: [None] * v.shape[0] for k, v in w.items()}
    for i in reversed(range(DEPTH)):
        j = i // 2
        tag = f"l{i}"
        cur_mx, mix, s_mix, mid_mx, hu, hg, act, s_ffn = saved[i]
        ds, ds_mx, g["ln_ffn_g"][i], g["ln_ffn_b"][i] = ln_bwd(dout, s_ffn, w["ln_ffn_g"][i], f"{tag}_ln_ffn_bwd")
        g["down"][i] = matmul(act, ds_mx, "tn", F32, f"{tag}_d_down")
        dact = matmul(ds_mx, w["down"][i], "nt", F32, f"{tag}_dact")
        dhu, dhg = conv_gate_bwd_act(dact, hu, hg, w["cw_u"][i], w["cw_g"][i], w["cb_u"][i], w["cb_g"][i],
                                     f"{tag}_gate_bwd")
        dh_u, g["cw_u"][i], g["cb_u"][i] = conv_bwd(dhu, hu, w["cw_u"][i], f"{tag}_conv_bwd_u")
        dh_g, g["cw_g"][i], g["cb_g"][i] = conv_bwd(dhg, hg, w["cw_g"][i], f"{tag}_conv_bwd_g")
        g["up_u"][i] = matmul(mid_mx, dh_u, "tn", F32, f"{tag}_d_up_u")
        g["up_g"][i] = matmul(mid_mx, dh_g, "tn", F32, f"{tag}_d_up_g")
        dmid = matmul(dh_u, w["up_u"][i], "nt", F32, f"{tag}_dmid_u", resid=ds, resid_scale=ALPHA)
        dmid = matmul(dh_g, w["up_g"][i], "nt", F32, f"{tag}_dmid_g", resid=dmid)
        ds2, ds2_mx, g["ln_mix_g"][i], g["ln_mix_b"][i] = ln_bwd(dmid, s_mix, w["ln_mix_g"][i], f"{tag}_ln_mix_bwd")
        if i % 2 == 0:
            qkv, o_mx, o_f32 = mix
            g["sb_o"][j] = matmul(o_mx, ds2_mx, "tn", F32, f"{tag}_d_sb_o")
            do = matmul(ds2_mx, w["sb_o"][j], "nt", MXU_DTYPE, f"{tag}_do")
            dq, dk, dv = sb_attention_bwd(qkv, do, o_f32, f"{tag}_sb_bwd")
            dqkv = jnp.concatenate([dq, dk, dv], axis=1)
            g["sb_qkv"][j] = matmul(cur_mx, dqkv, "tn", F32, f"{tag}_d_qkv")
            dout = matmul(dqkv, w["sb_qkv"][j], "nt", F32, f"{tag}_dx", resid=ds2, resid_scale=ALPHA)
        else:
            proj, og, o_pre, states = mix
            g["gla_o"][j] = matmul(og, ds2_mx, "tn", F32, f"{tag}_d_gla_o")
            dog = matmul(ds2_mx, w["gla_o"][j], "nt", F32, f"{tag}_dog")
            dproj, g["gla_a2"][j], g["gla_ba"][j], g["gla_ng"][j] = gla_bwd(
                proj, w["gla_a2"][j], w["gla_ba"][j], w["gla_ng"][j], o_pre, states, dog, heads, f"{tag}_gla_bwd")
            g["gla_in"][j] = matmul(cur_mx, dproj, "tn", F32, f"{tag}_d_gla_in")
            dout = matmul(dproj, w["gla_in"][j], "nt", F32, f"{tag}_dx", resid=ds2, resid_scale=ALPHA)
    grads = {k: jnp.stack(v) for k, v in g.items()}
    return loss_cols, dout, grads


MESH_IDS = pl.DeviceIdType.MESH
ANY = pl.BlockSpec(memory_space=pl.ANY)


def _place():
    return lax.axis_index("x"), lax.axis_index("y"), lax.axis_index("c")


def all_gather(shard, name):
    r, c = shard.shape

    def body(x_ref, out_ref, send_sems, recv_sems, local_sem):
        x, y, cc = _place()
        me, sibling = (x, y, cc), (x, y, 1 - cc)
        chips = [(1 - x, y), (x, 1 - y), (1 - x, 1 - y)]

        def slot(px, py, pc):
            return out_ref.at[4 * px + 2 * py + pc]

        def copy(k, block, to, src=None):
            return pltpu.make_async_remote_copy(
                src_ref=slot(*block) if src is None else src, dst_ref=slot(*block),
                send_sem=send_sems.at[k], recv_sem=recv_sems.at[k], device_id=to, device_id_type=MESH_IDS)

        mine = pltpu.make_async_copy(x_ref, slot(*me), local_sem)
        mine.start()
        first = [copy(0, me, sibling, src=x_ref)]
        first += [copy(1 + j, me, (*chip, cc), src=x_ref) for j, chip in enumerate(chips)]
        for cp in first:
            cp.start()
        passed = [copy(4 + j, (*chip, cc), sibling) for j, chip in enumerate(chips)]
        for j, chip in enumerate(chips):
            copy(1 + j, (*chip, cc), me).wait_recv()
            passed[j].start()
        copy(0, sibling, me).wait_recv()
        for j, chip in enumerate(chips):
            copy(4 + j, (*chip, 1 - cc), me).wait_recv()
        for cp in first + passed:
            cp.wait_send()
        mine.wait()

    return pl.pallas_call(
        body, name=name, in_specs=[ANY], out_specs=ANY,
        out_shape=jax.ShapeDtypeStruct((N_DEV, r, c), shard.dtype),
        scratch_shapes=[pltpu.SemaphoreType.DMA((7,)), pltpu.SemaphoreType.DMA((7,)), pltpu.SemaphoreType.DMA],
    )(shard)


def exchange_partials(parts, name):
    _, r, c = parts.shape

    def body(p_ref, out_ref, send_sems, recv_sems, local_sem):
        x, y, cc = _place()
        my_slot = 4 * x + 2 * y + cc

        def peer(m):
            return (x ^ ((m >> 2) & 1), y ^ ((m >> 1) & 1), cc ^ (m & 1))

        def copy(m):
            px, py, pc = peer(m)
            return pltpu.make_async_remote_copy(
                src_ref=p_ref.at[4 * px + 2 * py + pc], dst_ref=out_ref.at[my_slot],
                send_sem=send_sems.at[m - 1], recv_sem=recv_sems.at[m - 1],
                device_id=(px, py, pc), device_id_type=MESH_IDS)

        def arrival(m):
            px, py, pc = peer(m)
            return pltpu.make_async_remote_copy(
                src_ref=p_ref.at[my_slot], dst_ref=out_ref.at[4 * px + 2 * py + pc],
                send_sem=send_sems.at[m - 1], recv_sem=recv_sems.at[m - 1],
                device_id=(px, py, pc), device_id_type=MESH_IDS)

        mine = pltpu.make_async_copy(p_ref.at[my_slot], out_ref.at[my_slot], local_sem)
        mine.start()
        sends = [copy(m) for m in range(1, N_DEV)]
        for cp in sends:
            cp.start()
        for m in range(1, N_DEV):
            arrival(m).wait_recv()
        for cp in sends:
            cp.wait_send()
        mine.wait()

    return pl.pallas_call(
        body, name=name, in_specs=[ANY], out_specs=ANY,
        out_shape=jax.ShapeDtypeStruct(parts.shape, parts.dtype),
        scratch_shapes=[pltpu.SemaphoreType.DMA((7,)), pltpu.SemaphoreType.DMA((7,)), pltpu.SemaphoreType.DMA],
    )(parts)


def sum_slots(parts, name):
    n, r, c = parts.shape
    tr = _pick(r, (256, 128, 64, 32, 16, 8))

    def body(p_ref, o_ref):
        acc = p_ref[0]
        for s in range(1, n):
            acc = acc + p_ref[s]
        o_ref[...] = acc

    return pl.pallas_call(
        body, name=name, grid=(r // tr,),
        in_specs=[pl.BlockSpec((n, tr, c), lambda i: (0, i, 0))],
        out_specs=pl.BlockSpec((tr, c), lambda i: (i, 0)),
        out_shape=jax.ShapeDtypeStruct((r, c), parts.dtype),
        compiler_params=_params(("parallel",)),
    )(parts)


def adamw(w, g, m, v, name):
    shape = w.shape
    cols = shape[-1]
    rows = math.prod(shape[:-1])
    tr = rows if rows <= 512 else _pick(rows, (512, 256, 128, 64, 32, 16, 8))
    two_d = lambda a: a.reshape(rows, cols)

    def body(w_ref, g_ref, m_ref, v_ref, d_ref, nm_ref, nv_ref):
        gv = g_ref[...]
        nm = ADAM_B1 * m_ref[...] + (1.0 - ADAM_B1) * gv
        nv = ADAM_B2 * v_ref[...] + (1.0 - ADAM_B2) * (gv * gv)
        m_hat = nm / (1.0 - ADAM_B1 ** ADAM_STEP)
        v_hat = nv / (1.0 - ADAM_B2 ** ADAM_STEP)
        d_ref[...] = -ADAM_LR * (m_hat / (jnp.sqrt(v_hat) + ADAM_EPS) + ADAM_WD * w_ref[...])
        nm_ref[...] = nm
        nv_ref[...] = nv

    spec = pl.BlockSpec((tr, cols), lambda i: (i, 0))
    out = pl.pallas_call(
        body, name=name, grid=(rows // tr,),
        in_specs=[spec] * 4, out_specs=[spec] * 3,
        out_shape=[jax.ShapeDtypeStruct((rows, cols), F32)] * 3,
        compiler_params=_params(("parallel",)),
    )(two_d(w), two_d(g), two_d(m), two_d(v))
    return tuple(o.reshape(shape) for o in out)


WEIGHT_NAMES = ("sb_w_qkv", "sb_w_o", "gla_w_in", "gla_w_a1", "gla_w_a2", "gla_b_a", "gla_norm_g", "gla_w_o",
                "ffn_w_up", "ffn_conv_w", "ffn_conv_b", "ffn_w_down", "ln_mix_g", "ln_mix_b", "ln_ffn_g", "ln_ffn_b")
SHARD_AXIS = {"sb_w_qkv": 2, "sb_w_o": 1, "gla_w_in": 2, "gla_w_a1": 1, "gla_w_a2": 2, "gla_b_a": 1,
              "gla_norm_g": 1, "gla_w_o": 1, "ffn_w_up": 2, "ffn_conv_w": 2, "ffn_w_down": 1}
MATMUL_WEIGHTS = ("sb_w_qkv", "sb_w_o", "gla_w_in", "gla_w_o", "ffn_w_up", "ffn_w_down")
SMALL_WEIGHTS = ("gla_w_a1", "gla_w_a2", "gla_b_a", "gla_norm_g", "ffn_conv_w")
REPLICATED = ("ffn_conv_b", "ln_mix_g", "ln_mix_b", "ln_ffn_g", "ln_ffn_b")


def _rows_of(n, row_multiple):
    rows = -(-n // PACK_COLS)
    return -(-rows // row_multiple) * row_multiple


def _pack(flat_parts, lead, row_multiple):
    out = []
    for p in flat_parts:
        n = p.shape[-1]
        rows = _rows_of(n, row_multiple)
        pad = rows * PACK_COLS - n
        if pad:
            p = jnp.pad(p, [(0, 0)] * len(lead) + [(0, pad)])
        out.append(p.reshape(*lead, rows, PACK_COLS))
    return jnp.concatenate(out, axis=len(lead))


def _unpack(buf, sizes, row_multiple):
    lead = buf.shape[:-2]
    out, row = [], 0
    for n in sizes:
        r = _rows_of(n, row_multiple)
        piece = lax.slice_in_dim(buf, row, row + r, axis=len(lead)).reshape(*lead, r * PACK_COLS)
        out.append(piece[..., :n])
        row += r
    return out


def _whole_from_shards(stacked, shard_shape, axis):
    a = jnp.moveaxis(stacked.reshape((N_DEV,) + tuple(shard_shape)), 0, axis)
    shape = list(shard_shape)
    shape[axis] *= N_DEV
    return a.reshape(shape)


def _shards_from_whole(whole, axis):
    shape = list(whole.shape)
    a = whole.reshape(shape[:axis] + [N_DEV, shape[axis] // N_DEV] + shape[axis + 1:])
    return jnp.moveaxis(a, axis, 0).reshape(N_DEV, -1)


def kernel(x, sb_w_qkv, sb_w_o, gla_w_in, gla_w_a1, gla_w_a2, gla_b_a, gla_norm_g, gla_w_o, ffn_w_up, ffn_conv_w, ffn_conv_b, ffn_w_down, ln_mix_g, ln_mix_b, ln_ffn_g, ln_ffn_b, loss_target, m_sb_w_qkv, m_sb_w_o, m_gla_w_in, m_gla_w_a1, m_gla_w_a2, m_gla_b_a, m_gla_norm_g, m_gla_w_o, m_ffn_w_up, m_ffn_conv_w, m_ffn_conv_b, m_ffn_w_down, m_ln_mix_g, m_ln_mix_b, m_ln_ffn_g, m_ln_ffn_b, v_sb_w_qkv, v_sb_w_o, v_gla_w_in, v_gla_w_a1, v_gla_w_a2, v_gla_b_a, v_gla_norm_g, v_gla_w_o, v_ffn_w_up, v_ffn_conv_w, v_ffn_conv_b, v_ffn_w_down, v_ln_mix_g, v_ln_mix_b, v_ln_ffn_g, v_ln_ffn_b):
    wts = dict(zip(WEIGHT_NAMES, (sb_w_qkv, sb_w_o, gla_w_in, gla_w_a1, gla_w_a2, gla_b_a, gla_norm_g, gla_w_o,
                                  ffn_w_up, ffn_conv_w, ffn_conv_b, ffn_w_down, ln_mix_g, ln_mix_b, ln_ffn_g, ln_ffn_b)))
    mom1 = dict(zip(WEIGHT_NAMES, (m_sb_w_qkv, m_sb_w_o, m_gla_w_in, m_gla_w_a1, m_gla_w_a2, m_gla_b_a, m_gla_norm_g,
                                   m_gla_w_o, m_ffn_w_up, m_ffn_conv_w, m_ffn_conv_b, m_ffn_w_down, m_ln_mix_g,
                                   m_ln_mix_b, m_ln_ffn_g, m_ln_ffn_b)))
    mom2 = dict(zip(WEIGHT_NAMES, (v_sb_w_qkv, v_sb_w_o, v_gla_w_in, v_gla_w_a1, v_gla_w_a2, v_gla_b_a, v_gla_norm_g,
                                   v_gla_w_o, v_ffn_w_up, v_ffn_conv_w, v_ffn_conv_b, v_ffn_w_down, v_ln_mix_g,
                                   v_ln_mix_b, v_ln_ffn_g, v_ln_ffn_b)))
    t, d = x.shape[1], x.shape[2]
    f = ffn_w_up.shape[2] * N_DEV // 2

    big = all_gather(_pack([wts[n].reshape(-1).astype(MXU_DTYPE) for n in MATMUL_WEIGHTS], (), BF16_ROWS), "gather_matmul_weights")
    small = all_gather(_pack([wts[n].reshape(-1) for n in SMALL_WEIGHTS], (), SUBLANES), "gather_small_weights")
    whole = {}
    for names, buf, mult in ((MATMUL_WEIGHTS, big, BF16_ROWS), (SMALL_WEIGHTS, small, SUBLANES)):
        for n, piece in zip(names, _unpack(buf, [wts[n].size for n in names], mult)):
            whole[n] = _whole_from_shards(piece, wts[n].shape, SHARD_AXIS[n])
    a1 = jnp.pad(whole["gla_w_a1"].astype(MXU_DTYPE), ((0, 0), (0, 0), (0, LANES - GLA_GATE_RANK)))
    w = {
        "sb_qkv": whole["sb_w_qkv"], "sb_o": whole["sb_w_o"],
        "gla_in": jnp.concatenate([whole["gla_w_in"], a1], axis=2), "gla_o": whole["gla_w_o"],
        "up_u": whole["ffn_w_up"][:, :, :f], "up_g": whole["ffn_w_up"][:, :, f:], "down": whole["ffn_w_down"],
        "gla_a2": jnp.pad(whole["gla_w_a2"], ((0, 0), (0, LANES - GLA_GATE_RANK), (0, 0))),
        "gla_ba": whole["gla_b_a"][:, None, :], "gla_ng": whole["gla_norm_g"][:, None, :],
        "cw_u": whole["ffn_conv_w"][:, :, :f], "cw_g": whole["ffn_conv_w"][:, :, f:],
        "cb_u": ffn_conv_b[:, None, :f], "cb_g": ffn_conv_b[:, None, f:],
        "ln_mix_g": ln_mix_g, "ln_mix_b": ln_mix_b, "ln_ffn_g": ln_ffn_g, "ln_ffn_b": ln_ffn_b,
    }

    loss_cols, grad_x, g = local_step(x[0], loss_target[0], w)
    loss = lax.psum(jnp.sum(loss_cols), ("x", "y", "c"))

    full = {
        "sb_w_qkv": g["sb_qkv"], "sb_w_o": g["sb_o"], "gla_w_in": g["gla_in"][:, :, :3 * d],
        "gla_w_a1": g["gla_in"][:, :, 3 * d:3 * d + GLA_GATE_RANK], "gla_w_a2": g["gla_a2"][:, :GLA_GATE_RANK, :],
        "gla_b_a": g["gla_ba"][:, 0], "gla_norm_g": g["gla_ng"][:, 0], "gla_w_o": g["gla_o"],
        "ffn_w_up": jnp.concatenate([g["up_u"], g["up_g"]], axis=2),
        "ffn_conv_w": jnp.concatenate([g["cw_u"], g["cw_g"]], axis=2),
        "ffn_w_down": g["down"],
    }
    sharded = MATMUL_WEIGHTS + SMALL_WEIGHTS
    parts = _pack([_shards_from_whole(full[n], SHARD_AXIS[n]) for n in sharded], (N_DEV,), SUBLANES)
    summed = sum_slots(exchange_partials(parts, "exchange_weight_grads"), "sum_weight_grads")
    grads = {n: piece.reshape(wts[n].shape) for n, piece in zip(sharded, _unpack(summed, [wts[n].size for n in sharded], SUBLANES))}

    rep = {
        "ffn_conv_b": jnp.concatenate([g["cb_u"], g["cb_g"]], axis=2)[:, 0],
        "ln_mix_g": g["ln_mix_g"][:, 0], "ln_mix_b": g["ln_mix_b"][:, 0],
        "ln_ffn_g": g["ln_ffn_g"][:, 0], "ln_ffn_b": g["ln_ffn_b"][:, 0],
    }
    rep_all = all_gather(_pack([rep[n].reshape(-1) for n in REPLICATED], (), SUBLANES), "gather_replicated_grads")
    rep_sum = sum_slots(rep_all, "sum_replicated_grads")
    for n, piece in zip(REPLICATED, _unpack(rep_sum, [wts[n].size for n in REPLICATED], SUBLANES)):
        grads[n] = piece.reshape(wts[n].shape)

    upd = {n: adamw(wts[n], grads[n], mom1[n], mom2[n], f"adamw_{n}") for n in WEIGHT_NAMES}
    return (loss, grad_x[None], *[grads[n] for n in WEIGHT_NAMES], *[upd[n][0] for n in WEIGHT_NAMES],
            *[upd[n][1] for n in WEIGHT_NAMES], *[upd[n][2] for n in WEIGHT_NAMES])
```

```python
import functools
import math

import jax
import jax.numpy as jnp
from jax import lax
from jax.experimental import pallas as pl
from jax.experimental.pallas import tpu as pltpu

F32 = jnp.float32
BF16 = jnp.bfloat16
MXU_DTYPE = jnp.bfloat16

DEPTH = 4
ALPHA = (2.0 * DEPTH) ** 0.25
SB_HEAD_DIM = 64
GLA_DK = 128
GLA_DV = 256
GLA_GATE_RANK = 16
GLA_GATE_TAU = 16.0
GLA_CHUNK = 64
LN_EPS = 1e-5
ADAM_LR = 0.001
ADAM_B1 = 0.9
ADAM_B2 = 0.999
ADAM_EPS = 1e-08
ADAM_WD = 0.01
ADAM_STEP = 10

LANES = 128
SUBLANES = 8
BF16_ROWS = 16
VMEM_LIMIT = 56 * 1024 * 1024
N_DEV = 8
PACK_COLS = 1024

NN = ((1,), (0,))
NT = ((1,), (1,))
TN = ((0,), (0,))


def _dot(a, b, dims):
    return lax.dot_general(a.astype(MXU_DTYPE), b.astype(MXU_DTYPE), (dims, ((), ())),
                           preferred_element_type=F32)


def _split_dot(x, u, dims):
    hi = x.astype(MXU_DTYPE)
    lo = (x - hi.astype(F32)).astype(MXU_DTYPE)
    return (lax.dot_general(hi, u, (dims, ((), ())), preferred_element_type=F32)
            + lax.dot_general(lo, u, (dims, ((), ())), preferred_element_type=F32))


def _split_dot_left(u, x, dims):
    hi = x.astype(MXU_DTYPE)
    lo = (x - hi.astype(F32)).astype(MXU_DTYPE)
    return (lax.dot_general(u, hi, (dims, ((), ())), preferred_element_type=F32)
            + lax.dot_general(u, lo, (dims, ((), ())), preferred_element_type=F32))


def _sigmoid(x):
    return 1.0 / (1.0 + jnp.exp(-x))


def _log_sigmoid(x):
    return jnp.minimum(x, 0.0) - jnp.log(1.0 + jnp.exp(-jnp.abs(x)))


def _pick(dim, candidates):
    for c in candidates:
        if dim % c == 0:
            return c
    raise ValueError(f"no tile for {dim} in {candidates}")


def _params(sem):
    return pltpu.CompilerParams(dimension_semantics=sem, vmem_limit_bytes=VMEM_LIMIT)


def matmul(a, b, mode, out_dtype, name, resid=None, resid_scale=1.0):
    if mode == "nn":
        (m, k), (k2, n) = a.shape, b.shape
    elif mode == "nt":
        (m, k), (n, k2) = a.shape, b.shape
    else:
        (k, m), (k2, n) = a.shape, b.shape
    assert k == k2, (a.shape, b.shape, mode)
    tm = _pick(m, (1024, 512, 256, 128))
    tn = _pick(n, (1024, 704 * 2, 640, 512, 256, 128))
    tk = _pick(k, (1024, 704 * 2, 640, 512, 256, 128))
    nk = k // tk
    dims = {"nn": NN, "nt": NT, "tn": TN}[mode]

    if mode == "tn":
        a_spec = pl.BlockSpec((tk, tm), lambda i, j, kk: (kk, i))
    else:
        a_spec = pl.BlockSpec((tm, tk), lambda i, j, kk: (i, kk))
    if mode == "nt":
        b_spec = pl.BlockSpec((tn, tk), lambda i, j, kk: (j, kk))
    else:
        b_spec = pl.BlockSpec((tk, tn), lambda i, j, kk: (kk, j))
    o_spec = pl.BlockSpec((tm, tn), lambda i, j, kk: (i, j))
    has_resid = resid is not None

    def body(*refs):
        if has_resid:
            a_ref, b_ref, r_ref, o_ref, acc_ref = refs
        else:
            a_ref, b_ref, o_ref, acc_ref = refs
        kk = pl.program_id(2)

        @pl.when(kk == 0)
        def _():
            acc_ref[...] = jnp.zeros_like(acc_ref)

        acc_ref[...] += _dot(a_ref[...], b_ref[...], dims)

        @pl.when(kk == nk - 1)
        def _():
            acc = acc_ref[...]
            if has_resid:
                acc = acc + resid_scale * r_ref[...]
            o_ref[...] = acc.astype(out_dtype)

    in_specs = [a_spec, b_spec] + ([o_spec] if has_resid else [])
    args = (a, b) + ((resid,) if has_resid else ())
    return pl.pallas_call(
        body, name=name, grid=(m // tm, n // tn, nk),
        in_specs=in_specs, out_specs=o_spec,
        out_shape=jax.ShapeDtypeStruct((m, n), out_dtype),
        scratch_shapes=[pltpu.VMEM((tm, tn), F32)],
        compiler_params=_params(("parallel", "parallel", "arbitrary")),
    )(*args)


def resid_ln_fwd(x, y, g, b, name):
    t, d = x.shape
    tm = _pick(t, (256, 128))

    def body(x_ref, y_ref, g_ref, b_ref, o_ref, ob_ref, s_ref):
        s = ALPHA * x_ref[...] + y_ref[...]
        mu = jnp.mean(s, axis=-1, keepdims=True)
        c = s - mu
        var = jnp.mean(c * c, axis=-1, keepdims=True)
        out = c * lax.rsqrt(var + LN_EPS) * g_ref[...] + b_ref[...]
        s_ref[...] = s
        o_ref[...] = out
        ob_ref[...] = out.astype(MXU_DTYPE)

    row = pl.BlockSpec((tm, d), lambda i: (i, 0))
    vec = pl.BlockSpec((1, d), lambda i: (0, 0))
    return pl.pallas_call(
        body, name=name, grid=(t // tm,),
        in_specs=[row, row, vec, vec], out_specs=[row, row, row],
        out_shape=[jax.ShapeDtypeStruct((t, d), F32), jax.ShapeDtypeStruct((t, d), MXU_DTYPE),
                   jax.ShapeDtypeStruct((t, d), F32)],
        compiler_params=_params(("parallel",)),
    )(x, y, g.reshape(1, d), b.reshape(1, d))


def ln_bwd(dout, s, g, name):
    t, d = s.shape
    tm = _pick(t, (256, 128))

    def body(do_ref, s_ref, g_ref, ds_ref, dsb_ref, dg_ref, db_ref):
        @pl.when(pl.program_id(0) == 0)
        def _():
            dg_ref[...] = jnp.zeros_like(dg_ref)
            db_ref[...] = jnp.zeros_like(db_ref)

        sv = s_ref[...]
        mu = jnp.mean(sv, axis=-1, keepdims=True)
        c = sv - mu
        var = jnp.mean(c * c, axis=-1, keepdims=True)
        rstd = lax.rsqrt(var + LN_EPS)
        xhat = c * rstd
        do = do_ref[...]
        dxh = do * g_ref[...]
        m1 = jnp.mean(dxh, axis=-1, keepdims=True)
        m2 = jnp.mean(dxh * xhat, axis=-1, keepdims=True)
        ds = rstd * (dxh - m1 - xhat * m2)
        ds_ref[...] = ds
        dsb_ref[...] = ds.astype(MXU_DTYPE)
        dg_ref[...] += jnp.sum(do * xhat, axis=0, keepdims=True)
        db_ref[...] += jnp.sum(do, axis=0, keepdims=True)

    row = pl.BlockSpec((tm, d), lambda i: (i, 0))
    vec = pl.BlockSpec((1, d), lambda i: (0, 0))
    return pl.pallas_call(
        body, name=name, grid=(t // tm,),
        in_specs=[row, row, vec], out_specs=[row, row, vec, vec],
        out_shape=[jax.ShapeDtypeStruct((t, d), F32), jax.ShapeDtypeStruct((t, d), MXU_DTYPE),
                   jax.ShapeDtypeStruct((1, d), F32), jax.ShapeDtypeStruct((1, d), F32)],
        compiler_params=_params(("arbitrary",)),
    )(dout, s, g.reshape(1, d))


def loss_and_grad(y, target, name):
    t, d = y.shape
    tm = _pick(t, (256, 128))

    def body(y_ref, t_ref, l_ref, dy_ref):
        @pl.when(pl.program_id(0) == 0)
        def _():
            l_ref[...] = jnp.zeros_like(l_ref)

        e = y_ref[...] - t_ref[...]
        dy_ref[...] = e / d
        l_ref[...] += jnp.sum(e * e, axis=0, keepdims=True) * (0.5 / d)

    row = pl.BlockSpec((tm, d), lambda i: (i, 0))
    vec = pl.BlockSpec((1, d), lambda i: (0, 0))
    return pl.pallas_call(
        body, name=name, grid=(t // tm,),
        in_specs=[row, row], out_specs=[vec, row],
        out_shape=[jax.ShapeDtypeStruct((1, d), F32), jax.ShapeDtypeStruct((t, d), F32)],
        compiler_params=_params(("arbitrary",)),
    )(y, target)


SB_TQ = 1024
SB_TQ_FWD = 2048
SB_TK = 128
SB_UNROLL = 2


def _suffix_matrix(tk, inclusive):
    r = lax.broadcasted_iota(jnp.int32, (tk, tk), 0)
    c = lax.broadcasted_iota(jnp.int32, (tk, tk), 1)
    return ((r >= c) if inclusive else (r > c)).astype(MXU_DTYPE)


def _sb_block(qa, kj, run_l, upper, diag_mask):
    z = _dot(qa, kj, NT)
    sp = jnp.log(1.0 + jnp.exp(-jnp.abs(z)))
    ls_pos = jnp.minimum(z, 0.0) - sp
    ls_neg = -jnp.maximum(z, 0.0) - sp
    if diag_mask is not None:
        ls_neg = jnp.where(diag_mask, ls_neg, 0.0)
    a = jnp.exp(ls_pos + _split_dot(ls_neg, upper, NN) + run_l)
    if diag_mask is not None:
        a = jnp.where(diag_mask, a, 0.0)
    return ls_pos, jnp.sum(ls_neg, axis=-1, keepdims=True), a


def sb_attention_fwd(qkv, name):
    t, d3 = qkv.shape
    d = d3 // 3
    npair = d // LANES
    tq, tk = min(SB_TQ_FWD, t), SB_TK
    nq, ratio = t // tq, tq // tk
    scale = SB_HEAD_DIM ** -0.5

    def body(q_ref, k_ref, v_ref, o_ref, of_ref, qs_ref, acc_ref, fine_ref, run_ref):
        i = pl.program_id(1)
        lane = lax.broadcasted_iota(jnp.int32, (1, LANES), 1)
        upper = _suffix_matrix(tk, inclusive=False)
        q2 = q_ref[...]
        for a in range(2):
            qs_ref[a] = jnp.where(lane // SB_HEAD_DIM == a, q2, jnp.zeros_like(q2)) * scale
        acc_ref[...] = jnp.zeros_like(acc_ref)
        fine_ref[...] = jnp.zeros_like(fine_ref)
        run_ref[...] = jnp.zeros_like(run_ref)

        def block(start, row0, mask):
            kj = k_ref[pl.ds(start, tk), :]
            vj = v_ref[pl.ds(start, tk), :]
            for a in range(2):
                _, total, p = _sb_block(qs_ref[a, row0:, :], kj, run_ref[a, row0:, :], upper, mask)
                p_mx = p.astype(MXU_DTYPE)
                acc_ref[a, row0:, :] += _dot(p_mx, vj, NN)
                fine_ref[a, row0:, :] += _dot(p - p_mx.astype(F32), vj, NN)
                run_ref[a, row0:, :] += total

        for back in range(ratio):
            row0 = (ratio - 1 - back) * tk
            rows = tq - row0
            mask = (lax.broadcasted_iota(jnp.int32, (rows, tk), 1)
                    < lax.broadcasted_iota(jnp.int32, (rows, tk), 0))
            block(pl.multiple_of(i * tq + row0, tk), row0, mask)

        def earlier(n, _):
            for u in range(SB_UNROLL):
                block(pl.multiple_of((i * ratio - 1 - SB_UNROLL * n - u) * tk, tk), 0, None)
            return 0

        lax.fori_loop(0, i * ratio // SB_UNROLL, earlier, 0)
        o = jnp.where(lane < SB_HEAD_DIM, acc_ref[0], acc_ref[1])
        of_ref[...] = o + jnp.where(lane < SB_HEAD_DIM, fine_ref[0], fine_ref[1])
        o_ref[...] = o.astype(MXU_DTYPE)

    q_spec = pl.BlockSpec((tq, LANES), lambda p, i: (i, p))
    k_spec = pl.BlockSpec((t, LANES), lambda p, i: (0, npair + p))
    v_spec = pl.BlockSpec((t, LANES), lambda p, i: (0, 2 * npair + p))
    o_spec = pl.BlockSpec((tq, LANES), lambda p, i: (i, p))
    return pl.pallas_call(
        body, name=name, grid=(npair, nq),
        in_specs=[q_spec, k_spec, v_spec], out_specs=[o_spec, o_spec],
        out_shape=[jax.ShapeDtypeStruct((t, d), MXU_DTYPE), jax.ShapeDtypeStruct((t, d), F32)],
        scratch_shapes=[pltpu.VMEM((2, tq, LANES), MXU_DTYPE), pltpu.VMEM((2, tq, LANES), F32),
                        pltpu.VMEM((2, tq, LANES), F32), pltpu.VMEM((2, tq, 1), F32)],
        compiler_params=_params(("parallel", "arbitrary")),
    )(qkv, qkv, qkv)


def sb_attention_bwd(qkv, do, o_f32, name):
    t, d3 = qkv.shape
    d = d3 // 3
    npair = d // LANES
    tq, tk = min(SB_TQ, t), SB_TK
    nq, ratio = t // tq, tq // tk
    scale = SB_HEAD_DIM ** -0.5

    def body(q_ref, k_ref, v_ref, do_ref, of_ref, dq_ref, dk_ref, dv_ref,
             dk_acc, dv_acc, qs_ref, dos_ref, qst_ref, dost_ref, dq_acc, etot_ref, runl_ref, rune_ref):
        i = pl.program_id(1)

        @pl.when(i == 0)
        def _():
            dk_acc[...] = jnp.zeros_like(dk_acc)
            dv_acc[...] = jnp.zeros_like(dv_acc)

        lane = lax.broadcasted_iota(jnp.int32, (1, LANES), 1)
        upper = _suffix_matrix(tk, inclusive=False)
        upper_incl = _suffix_matrix(tk, inclusive=True)
        q2 = q_ref[...].astype(F32)
        do2 = do_ref[...].astype(F32)
        prod = do2 * of_ref[...]
        for a in range(2):
            head = lane // SB_HEAD_DIM == a
            qa = jnp.where(head, q2, 0.0) * scale
            doa = jnp.where(head, do2, 0.0)
            qs_ref[a] = qa.astype(MXU_DTYPE)
            dos_ref[a] = doa.astype(MXU_DTYPE)
            qst_ref[a] = jnp.transpose(qa).astype(MXU_DTYPE)
            dost_ref[a] = jnp.transpose(doa).astype(MXU_DTYPE)
            etot_ref[a] = jnp.sum(jnp.where(head, prod, 0.0), axis=-1, keepdims=True)
        dq_acc[...] = jnp.zeros_like(dq_acc)
        runl_ref[...] = jnp.zeros_like(runl_ref)
        rune_ref[...] = jnp.zeros_like(rune_ref)

        def block(j, row0, mask):
            start = pl.multiple_of(j * tk, tk)
            kj = k_ref[pl.ds(start, tk), :]
            vj = v_ref[pl.ds(start, tk), :]
            dk_blk = jnp.zeros((LANES, tk), F32)
            dv_blk = jnp.zeros((LANES, tk), F32)
            for a in range(2):
                qa = qs_ref[a, row0:, :]
                doa = dos_ref[a, row0:, :]
                ls_pos, total_l, p = _sb_block(qa, kj, runl_ref[a, row0:, :], upper, mask)
                beta = jnp.exp(ls_pos)
                e = p * _dot(doa, vj, NT)
                q_incl = _split_dot(e, upper_incl, NN) + rune_ref[a, row0:, :]
                dz = e * (1.0 - beta) - beta * (etot_ref[a, row0:, :] - q_incl)
                if mask is not None:
                    dz = jnp.where(mask, dz, 0.0)
                dz = dz.astype(MXU_DTYPE)
                dq_acc[a, row0:, :] += _dot(dz, kj, NN)
                dk_blk = dk_blk + _dot(qst_ref[a, :, row0:], dz, NN)
                dv_blk = dv_blk + _dot(dost_ref[a, :, row0:], p, NN)
                runl_ref[a, row0:, :] += total_l
                rune_ref[a, row0:, :] += jnp.sum(e, axis=-1, keepdims=True)
            dk_acc[j] += dk_blk
            dv_acc[j] += dv_blk

        for back in range(ratio):
            row0 = (ratio - 1 - back) * tk
            rows = tq - row0
            mask = (lax.broadcasted_iota(jnp.int32, (rows, tk), 1)
                    < lax.broadcasted_iota(jnp.int32, (rows, tk), 0))
            block(i * ratio + ratio - 1 - back, row0, mask)

        def earlier(n, _):
            for u in range(SB_UNROLL):
                block(i * ratio - 1 - SB_UNROLL * n - u, 0, None)
            return 0

        lax.fori_loop(0, i * ratio // SB_UNROLL, earlier, 0)
        dq = jnp.where(lane < SB_HEAD_DIM, dq_acc[0], dq_acc[1]) * scale
        dq_ref[...] = dq.astype(MXU_DTYPE)

        @pl.when(i == nq - 1)
        def _():
            def untranspose(j, _):
                rows = pl.ds(pl.multiple_of(j * tk, tk), tk)
                dk_ref[rows, :] = jnp.transpose(dk_acc[j]).astype(MXU_DTYPE)
                dv_ref[rows, :] = jnp.transpose(dv_acc[j]).astype(MXU_DTYPE)
                return 0

            lax.fori_loop(0, t // tk, untranspose, 0)

    q_spec = pl.BlockSpec((tq, LANES), lambda p, i: (i, p))
    k_spec = pl.BlockSpec((t, LANES), lambda p, i: (0, npair + p))
    v_spec = pl.BlockSpec((t, LANES), lambda p, i: (0, 2 * npair + p))
    full = pl.BlockSpec((t, LANES), lambda p, i: (0, p))
    return pl.pallas_call(
        body, name=name, grid=(npair, nq),
        in_specs=[q_spec, k_spec, v_spec, q_spec, q_spec], out_specs=[q_spec, full, full],
        out_shape=[jax.ShapeDtypeStruct((t, d), MXU_DTYPE)] * 3,
        scratch_shapes=[pltpu.VMEM((t // tk, LANES, tk), F32), pltpu.VMEM((t // tk, LANES, tk), F32),
                        pltpu.VMEM((2, tq, LANES), MXU_DTYPE), pltpu.VMEM((2, tq, LANES), MXU_DTYPE),
                        pltpu.VMEM((2, LANES, tq), MXU_DTYPE), pltpu.VMEM((2, LANES, tq), MXU_DTYPE),
                        pltpu.VMEM((2, tq, LANES), F32), pltpu.VMEM((2, tq, 1), F32),
                        pltpu.VMEM((2, tq, 1), F32), pltpu.VMEM((2, tq, 1), F32)],
        compiler_params=_params(("parallel", "arbitrary")),
    )(qkv, qkv, qkv, do, o_f32)


GLA_ROWS = 256


def _gla_cols(heads):
    kd, vd = heads * GLA_DK, heads * GLA_DV
    return kd, vd, 2 * kd, 2 * kd + vd, 2 * kd + 2 * vd


def _gla_chunk(proj_ref, rows, h, heads, wa2, ba):
    kd, vd, v0, r0, g0 = _gla_cols(heads)
    c = GLA_CHUNK
    q = proj_ref[rows, h * GLA_DK:(h + 1) * GLA_DK]
    k = proj_ref[rows, kd + h * GLA_DK:kd + (h + 1) * GLA_DK]
    v = proj_ref[rows, v0 + h * GLA_DV:v0 + (h + 1) * GLA_DV]
    xa = proj_ref[rows, g0:g0 + LANES]
    gp = _dot(xa, wa2, NN) + ba
    g = _log_sigmoid(gp) / GLA_GATE_TAU
    r_i = lax.broadcasted_iota(jnp.int32, (c, c), 0)
    c_i = lax.broadcasted_iota(jnp.int32, (c, c), 1)
    causal = c_i <= r_i
    b = _split_dot_left(causal.astype(MXU_DTYPE), g, NN)
    bl = b[c - 1:c, :]
    qd = q * (GLA_DK ** -0.5) * jnp.exp(b)
    ki = k * jnp.exp(-b)
    kdec = k * jnp.exp(bl - b)
    attn = jnp.where(causal, _dot(qd, ki, NT), 0.0)
    return q, k, v, xa, gp, b, bl, qd, ki, kdec, attn, causal


def gla_fwd(proj, wa2, ba, norm_g, heads, name):
    t, width = proj.shape
    kd, vd, v0, r0, g0 = _gla_cols(heads)
    assert width == g0 + LANES
    c = GLA_CHUNK
    rg = min(GLA_ROWS, t)
    ncs = rg // c

    def body(proj_ref, wa2_ref, ba_ref, ng_ref, og_ref, o_ref, st_ref, state):
        @pl.when(pl.program_id(0) == 0)
        def _():
            state[...] = jnp.zeros_like(state)

        ng = ng_ref[...]
        for h in range(heads):
            wa2_h = wa2_ref[:, h * GLA_DK:(h + 1) * GLA_DK]
            ba_h = ba_ref[:, h * GLA_DK:(h + 1) * GLA_DK]

            def chunk(n, _):
                rows = pl.ds(pl.multiple_of(n * c, c), c)
                q, k, v, xa, gp, b, bl, qd, ki, kdec, attn, _ = _gla_chunk(proj_ref, rows, h, heads, wa2_h, ba_h)
                s_t = state[h]
                st_ref[n, h] = s_t
                o = _dot(attn, v, NN) + _dot(qd, s_t, NT)
                state[h] = s_t * jnp.exp(bl) + _dot(v, kdec, TN)
                o_ref[rows, h * GLA_DV:(h + 1) * GLA_DV] = o
                rstd = lax.rsqrt(jnp.mean(o * o, axis=-1, keepdims=True) + LN_EPS)
                r = proj_ref[rows, r0 + h * GLA_DV:r0 + (h + 1) * GLA_DV]
                og = o * rstd * ng * (r * _sigmoid(r))
                og_ref[rows, h * GLA_DV:(h + 1) * GLA_DV] = og.astype(MXU_DTYPE)
                return 0

            lax.fori_loop(0, ncs, chunk, 0)

    return pl.pallas_call(
        body, name=name, grid=(t // rg,),
        in_specs=[pl.BlockSpec((rg, width), lambda i: (i, 0)),
                  pl.BlockSpec((LANES, kd), lambda i: (0, 0)),
                  pl.BlockSpec((1, kd), lambda i: (0, 0)),
                  pl.BlockSpec((1, GLA_DV), lambda i: (0, 0))],
        out_specs=[pl.BlockSpec((rg, vd), lambda i: (i, 0)),
                   pl.BlockSpec((rg, vd), lambda i: (i, 0)),
                   pl.BlockSpec((ncs, heads, GLA_DV, GLA_DK), lambda i: (i, 0, 0, 0))],
        out_shape=[jax.ShapeDtypeStruct((t, vd), MXU_DTYPE), jax.ShapeDtypeStruct((t, vd), F32),
                   jax.ShapeDtypeStruct((t // c, heads, GLA_DV, GLA_DK), F32)],
        scratch_shapes=[pltpu.VMEM((heads, GLA_DV, GLA_DK), F32)],
        compiler_params=_params(("arbitrary",)),
    )(proj, wa2, ba, norm_g)


def gla_bwd(proj, wa2, ba, norm_g, o_pre, states, dog, heads, name):
    t, width = proj.shape
    kd, vd, v0, r0, g0 = _gla_cols(heads)
    c = GLA_CHUNK
    rg = min(GLA_ROWS, t)
    ncs = rg // c
    ngrp = t // rg

    def body(proj_ref, wa2_ref, ba_ref, ng_ref, o_ref, st_ref, dog_ref,
             dproj_ref, dwa2_ref, dba_ref, dng_ref, dstate):
        @pl.when(pl.program_id(0) == 0)
        def _():
            dstate[...] = jnp.zeros_like(dstate)
            dwa2_ref[...] = jnp.zeros_like(dwa2_ref)
            dba_ref[...] = jnp.zeros_like(dba_ref)
            dng_ref[...] = jnp.zeros_like(dng_ref)

        ng = ng_ref[...]
        dproj_ref[:, g0:g0 + LANES] = jnp.zeros((rg, LANES), F32)
        for h in range(heads):
            wa2_h = wa2_ref[:, h * GLA_DK:(h + 1) * GLA_DK]
            ba_h = ba_ref[:, h * GLA_DK:(h + 1) * GLA_DK]

            def chunk(m, _):
                n = ncs - 1 - m
                rows = pl.ds(pl.multiple_of(n * c, c), c)
                q, k, v, xa, gp, b, bl, qd, ki, kdec, attn, causal = _gla_chunk(
                    proj_ref, rows, h, heads, wa2_h, ba_h)
                s_t = st_ref[n, h]
                ds_t = dstate[h]
                o = o_ref[rows, h * GLA_DV:(h + 1) * GLA_DV]
                r = proj_ref[rows, r0 + h * GLA_DV:r0 + (h + 1) * GLA_DV]
                dg_out = dog_ref[rows, h * GLA_DV:(h + 1) * GLA_DV]
                rstd = lax.rsqrt(jnp.mean(o * o, axis=-1, keepdims=True) + LN_EPS)
                ohat = o * rstd
                sig = _sigmoid(r)
                silu = r * sig
                dn = dg_out * silu
                dr = dg_out * (ohat * ng) * (sig * (1.0 + r * (1.0 - sig)))
                dng_ref[...] += jnp.sum(dn * ohat, axis=0, keepdims=True)
                dohat = dn * ng
                d_o = rstd * (dohat - ohat * jnp.mean(dohat * ohat, axis=-1, keepdims=True))
                d_attn = jnp.where(causal, _dot(d_o, v, NT), 0.0)
                dv = _dot(attn, d_o, TN) + _dot(kdec, ds_t, NT)
                dqd = _dot(d_attn, ki, NN) + _dot(d_o, s_t, NN)
                dki = _dot(d_attn, qd, TN)
                dkdec = _dot(v, ds_t, NN)
                decay = jnp.exp(bl)
                dbl = jnp.sum(ds_t * s_t, axis=0, keepdims=True) * decay
                dstate[h] = ds_t * decay + _dot(d_o, qd, TN)
                dq = dqd * (GLA_DK ** -0.5) * jnp.exp(b)
                dk = dki * jnp.exp(-b) + dkdec * jnp.exp(bl - b)
                db = dqd * qd - dki * ki - dkdec * kdec
                dbl = dbl + jnp.sum(dkdec * kdec, axis=0, keepdims=True)
                last = lax.broadcasted_iota(jnp.int32, (c, 1), 0) == c - 1
                db = db + jnp.where(last, dbl, 0.0)
                r_i = lax.broadcasted_iota(jnp.int32, (c, c), 0)
                c_i = lax.broadcasted_iota(jnp.int32, (c, c), 1)
                dg = _split_dot_left((c_i >= r_i).astype(MXU_DTYPE), db, NN)
                dgp = dg * (1.0 / GLA_GATE_TAU) * _sigmoid(-gp)
                dproj_ref[rows, h * GLA_DK:(h + 1) * GLA_DK] = dq
                dproj_ref[rows, kd + h * GLA_DK:kd + (h + 1) * GLA_DK] = dk
                dproj_ref[rows, v0 + h * GLA_DV:v0 + (h + 1) * GLA_DV] = dv
                dproj_ref[rows, r0 + h * GLA_DV:r0 + (h + 1) * GLA_DV] = dr
                dproj_ref[rows, g0:g0 + LANES] += _dot(dgp, wa2_h, NT)
                dwa2_ref[:, h * GLA_DK:(h + 1) * GLA_DK] += _dot(xa, dgp, TN)
                dba_ref[:, h * GLA_DK:(h + 1) * GLA_DK] += jnp.sum(dgp, axis=0, keepdims=True)
                return 0

            lax.fori_loop(0, ncs, chunk, 0)

    rev = lambda i: (ngrp - 1 - i, 0)
    return pl.pallas_call(
        body, name=name, grid=(ngrp,),
        in_specs=[pl.BlockSpec((rg, width), rev),
                  pl.BlockSpec((LANES, kd), lambda i: (0, 0)),
                  pl.BlockSpec((1, kd), lambda i: (0, 0)),
                  pl.BlockSpec((1, GLA_DV), lambda i: (0, 0)),
                  pl.BlockSpec((rg, vd), rev),
                  pl.BlockSpec((ncs, heads, GLA_DV, GLA_DK), lambda i: (ngrp - 1 - i, 0, 0, 0)),
                  pl.BlockSpec((rg, vd), rev)],
        out_specs=[pl.BlockSpec((rg, width), rev),
                   pl.BlockSpec((LANES, kd), lambda i: (0, 0)),
                   pl.BlockSpec((1, kd), lambda i: (0, 0)),
                   pl.BlockSpec((1, GLA_DV), lambda i: (0, 0))],
        out_shape=[jax.ShapeDtypeStruct((t, width), F32), jax.ShapeDtypeStruct((LANES, kd), F32),
                   jax.ShapeDtypeStruct((1, kd), F32), jax.ShapeDtypeStruct((1, GLA_DV), F32)],
        scratch_shapes=[pltpu.VMEM((heads, GLA_DV, GLA_DK), F32)],
        compiler_params=_params(("arbitrary",)),
    )(proj, wa2, ba, norm_g, o_pre, states, dog)


CONV_ROWS = 512
CONV_COLS = 256


def _shift_down(xv, halo_rows, nshift):
    rolled = pltpu.roll(xv, nshift, axis=0)
    ridx = lax.broadcasted_iota(jnp.int32, xv.shape, 0)
    out = rolled
    for r in range(nshift):
        src = halo_rows[halo_rows.shape[0] - nshift + r:halo_rows.shape[0] - nshift + r + 1, :]
        out = jnp.where(ridx == r, src, out)
    return out


def _shift_up(xv, halo_rows, nshift):
    n = xv.shape[0]
    rolled = pltpu.roll(xv, n - nshift, axis=0)
    ridx = lax.broadcasted_iota(jnp.int32, xv.shape, 0)
    out = rolled
    for r in range(nshift):
        out = jnp.where(ridx == n - nshift + r, halo_rows[r:r + 1, :], out)
    return out


def _mxu_rounded(v):
    return v.astype(MXU_DTYPE).astype(F32)


def _conv3(hv, prev, cw, cb):
    cw = _mxu_rounded(cw)
    return cw[0:1, :] * _shift_down(hv, prev, 2) + cw[1:2, :] * _shift_down(hv, prev, 1) + cw[2:3, :] * hv + cb


def _conv_specs(t, f):
    tm = _pick(t, (CONV_ROWS, 256, 128))
    tc = _pick(f, (CONV_COLS, 128))
    hb = BF16_ROWS
    nrow, ncol = t // tm, f // tc
    per = tm // hb
    tile = lambda off: pl.BlockSpec((tm, tc), lambda j, i: (i, off + j))
    prev = lambda off: pl.BlockSpec((hb, tc), lambda j, i: (jnp.maximum(i * per - 1, 0), off + j))
    nxt = lambda off: pl.BlockSpec((hb, tc), lambda j, i: (jnp.minimum((i + 1) * per, t // hb - 1), off + j))
    vec = lambda rows, off: pl.BlockSpec((rows, tc), lambda j, i: (0, off + j))
    return tm, tc, nrow, ncol, tile, prev, nxt, vec


def conv_gate_fwd(hu, hg, cwu, cwg, cbu, cbg, name):
    t, f = hu.shape
    tm, tc, nrow, ncol, tile, prev, nxt, vec = _conv_specs(t, f)

    def body(hu_ref, hg_ref, pu_ref, pg_ref, cwu_ref, cwg_ref, cbu_ref, cbg_ref, act_ref):
        first = pl.program_id(1) == 0
        pu = jnp.where(first, 0.0, pu_ref[...].astype(F32))
        pg = jnp.where(first, 0.0, pg_ref[...].astype(F32))
        u = _conv3(hu_ref[...].astype(F32), pu, cwu_ref[...], cbu_ref[...])
        g = _conv3(hg_ref[...].astype(F32), pg, cwg_ref[...], cbg_ref[...])
        act_ref[...] = (g * _sigmoid(g) * u).astype(MXU_DTYPE)

    return pl.pallas_call(
        body, name=name, grid=(ncol, nrow),
        in_specs=[tile(0), tile(0), prev(0), prev(0), vec(3, 0), vec(3, 0), vec(1, 0), vec(1, 0)],
        out_specs=tile(0),
        out_shape=jax.ShapeDtypeStruct((t, f), MXU_DTYPE),
        compiler_params=_params(("parallel", "arbitrary")),
    )(hu, hg, hu, hg, cwu, cwg, cbu, cbg)


def conv_gate_bwd_act(dact, hu, hg, cwu, cwg, cbu, cbg, name):
    t, f = hu.shape
    tm, tc, nrow, ncol, tile, prev, nxt, vec = _conv_specs(t, f)

    def body(da_ref, hu_ref, hg_ref, pu_ref, pg_ref, cwu_ref, cwg_ref, cbu_ref, cbg_ref, du_ref, dg_ref):
        first = pl.program_id(1) == 0
        pu = jnp.where(first, 0.0, pu_ref[...].astype(F32))
        pg = jnp.where(first, 0.0, pg_ref[...].astype(F32))
        u = _conv3(hu_ref[...].astype(F32), pu, cwu_ref[...], cbu_ref[...])
        g = _conv3(hg_ref[...].astype(F32), pg, cwg_ref[...], cbg_ref[...])
        da = da_ref[...].astype(F32)
        sig = _sigmoid(g)
        du_ref[...] = da * (g * sig)
        dg_ref[...] = da * u * (sig * (1.0 + g * (1.0 - sig)))

    out = pl.pallas_call(
        body, name=name, grid=(ncol, nrow),
        in_specs=[tile(0), tile(0), tile(0), prev(0), prev(0),
                  vec(3, 0), vec(3, 0), vec(1, 0), vec(1, 0)],
        out_specs=[tile(0), tile(0)],
        out_shape=[jax.ShapeDtypeStruct((t, f), F32), jax.ShapeDtypeStruct((t, f), F32)],
        compiler_params=_params(("parallel", "arbitrary")),
    )(dact, hu, hg, hu, hg, cwu, cwg, cbu, cbg)
    return out


def conv_bwd(dhc, h, cw, name):
    t, f = dhc.shape
    tm, tc, nrow, ncol, tile, prev, nxt, vec = _conv_specs(t, f)
    hb = BF16_ROWS
    per = tm // hb
    nxt32 = pl.BlockSpec((hb, tc), lambda j, i: (jnp.minimum((i + 1) * per, t // hb - 1), j))

    def body(d_ref, dn_ref, h_ref, hp_ref, cw_ref, dh_ref, dcw_ref, dcb_ref):
        i = pl.program_id(1)

        @pl.when(i == 0)
        def _():
            dcw_ref[...] = jnp.zeros_like(dcw_ref)
            dcb_ref[...] = jnp.zeros_like(dcb_ref)

        dcb_ref[...] += jnp.sum(d_ref[...], axis=0, keepdims=True)
        dv = _mxu_rounded(d_ref[...])
        nx = jnp.where(i == nrow - 1, 0.0, _mxu_rounded(dn_ref[...]))
        hp = jnp.where(i == 0, 0.0, hp_ref[...].astype(F32))
        hv = h_ref[...].astype(F32)
        cwv = _mxu_rounded(cw_ref[...])
        dh = cwv[2:3, :] * dv + cwv[1:2, :] * _shift_up(dv, nx, 1) + cwv[0:1, :] * _shift_up(dv, nx, 2)
        dh_ref[...] = dh.astype(MXU_DTYPE)
        dcw_ref[0:1, :] += jnp.sum(dv * _shift_down(hv, hp, 2), axis=0, keepdims=True)
        dcw_ref[1:2, :] += jnp.sum(dv * _shift_down(hv, hp, 1), axis=0, keepdims=True)
        dcw_ref[2:3, :] += jnp.sum(dv * hv, axis=0, keepdims=True)

    return pl.pallas_call(
        body, name=name, grid=(ncol, nrow),
        in_specs=[tile(0), nxt32, tile(0), prev(0), vec(3, 0)],
        out_specs=[tile(0), vec(3, 0), vec(1, 0)],
        out_shape=[jax.ShapeDtypeStruct((t, f), MXU_DTYPE), jax.ShapeDtypeStruct((3, f), F32),
                   jax.ShapeDtypeStruct((1, f), F32)],
        compiler_params=_params(("parallel", "arbitrary")),
    )(dhc, dhc, h, h, cw)


def local_step(x, target, w):
    t, d = x.shape
    heads = d // GLA_DV
    saved = []
    cur, cur_mx = x, x
    for i in range(DEPTH):
        j = i // 2
        tag = f"l{i}"
        if i % 2 == 0:
            qkv = matmul(cur_mx, w["sb_qkv"][j], "nn", MXU_DTYPE, f"{tag}_qkv")
            o_mx, o_f32 = sb_attention_fwd(qkv, f"{tag}_sb_fwd")
            y = matmul(o_mx, w["sb_o"][j], "nn", F32, f"{tag}_sb_out")
            mix = (qkv, o_mx, o_f32)
        else:
            proj = matmul(cur_mx, w["gla_in"][j], "nn", F32, f"{tag}_gla_in")
            og, o_pre, states = gla_fwd(proj, w["gla_a2"][j], w["gla_ba"][j], w["gla_ng"][j], heads, f"{tag}_gla_fwd")
            y = matmul(og, w["gla_o"][j], "nn", F32, f"{tag}_gla_out")
            mix = (proj, og, o_pre, states)
        mid, mid_mx, s_mix = resid_ln_fwd(cur, y, w["ln_mix_g"][i], w["ln_mix_b"][i], f"{tag}_ln_mix")
        hu = matmul(mid_mx, w["up_u"][i], "nn", MXU_DTYPE, f"{tag}_up_u")
        hg = matmul(mid_mx, w["up_g"][i], "nn", MXU_DTYPE, f"{tag}_up_g")
        act = conv_gate_fwd(hu, hg, w["cw_u"][i], w["cw_g"][i], w["cb_u"][i], w["cb_g"][i], f"{tag}_conv_fwd")
        y2 = matmul(act, w["down"][i], "nn", F32, f"{tag}_down")
        nxt, nxt_mx, s_ffn = resid_ln_fwd(mid, y2, w["ln_ffn_g"][i], w["ln_ffn_b"][i], f"{tag}_ln_ffn")
        saved.append((cur_mx, mix, s_mix, mid_mx, hu, hg, act, s_ffn))
        cur, cur_mx = nxt, nxt_mx

    loss_cols, dout = loss_and_grad(cur, target, "loss")

    g = {k: [None] * v.shape[0] for k, v in w.items()}
    for i in reversed(range(DEPTH)):
        j = i // 2
        tag = f"l{i}"
        cur_mx, mix, s_mix, mid_mx, hu, hg, act, s_ffn = saved[i]
        ds, ds_mx, g["ln_ffn_g"][i], g["ln_ffn_b"][i] = ln_bwd(dout, s_ffn, w["ln_ffn_g"][i], f"{tag}_ln_ffn_bwd")
        g["down"][i] = matmul(act, ds_mx, "tn", F32, f"{tag}_d_down")
        dact = matmul(ds_mx, w["down"][i], "nt", F32, f"{tag}_dact")
        dhu, dhg = conv_gate_bwd_act(dact, hu, hg, w["cw_u"][i], w["cw_g"][i], w["cb_u"][i], w["cb_g"][i],
                                     f"{tag}_gate_bwd")
        dh_u, g["cw_u"][i], g["cb_u"][i] = conv_bwd(dhu, hu, w["cw_u"][i], f"{tag}_conv_bwd_u")
        dh_g, g["cw_g"][i], g["cb_g"][i] = conv_bwd(dhg, hg, w["cw_g"][i], f"{tag}_conv_bwd_g")
        g["up_u"][i] = matmul(mid_mx, dh_u, "tn", F32, f"{tag}_d_up_u")
        g["up_g"][i] = matmul(mid_mx, dh_g, "tn", F32, f"{tag}_d_up_g")
        dmid = matmul(dh_u, w["up_u"][i], "nt", F32, f"{tag}_dmid_u", resid=ds, resid_scale=ALPHA)
        dmid = matmul(dh_g, w["up_g"][i], "nt", F32, f"{tag}_dmid_g", resid=dmid)
        ds2, ds2_mx, g["ln_mix_g"][i], g["ln_mix_b"][i] = ln_bwd(dmid, s_mix, w["ln_mix_g"][i], f"{tag}_ln_mix_bwd")
        if i % 2 == 0:
            qkv, o_mx, o_f32 = mix
            g["sb_o"][j] = matmul(o_mx, ds2_mx, "tn", F32, f"{tag}_d_sb_o")
            do = matmul(ds2_mx, w["sb_o"][j], "nt", MXU_DTYPE, f"{tag}_do")
            dq, dk, dv = sb_attention_bwd(qkv, do, o_f32, f"{tag}_sb_bwd")
            dqkv = jnp.concatenate([dq, dk, dv], axis=1)
            g["sb_qkv"][j] = matmul(cur_mx, dqkv, "tn", F32, f"{tag}_d_qkv")
            dout = matmul(dqkv, w["sb_qkv"][j], "nt", F32, f"{tag}_dx", resid=ds2, resid_scale=ALPHA)
        else:
            proj, og, o_pre, states = mix
            g["gla_o"][j] = matmul(og, ds2_mx, "tn", F32, f"{tag}_d_gla_o")
            dog = matmul(ds2_mx, w["gla_o"][j], "nt", F32, f"{tag}_dog")
            dproj, g["gla_a2"][j], g["gla_ba"][j], g["gla_ng"][j] = gla_bwd(
                proj, w["gla_a2"][j], w["gla_ba"][j], w["gla_ng"][j], o_pre, states, dog, heads, f"{tag}_gla_bwd")
            g["gla_in"][j] = matmul(cur_mx, dproj, "tn", F32, f"{tag}_d_gla_in")
            dout = matmul(dproj, w["gla_in"][j], "nt", F32, f"{tag}_dx", resid=ds2, resid_scale=ALPHA)
    grads = {k: jnp.stack(v) for k, v in g.items()}
    return loss_cols, dout, grads


MESH_IDS = pl.DeviceIdType.MESH
ANY = pl.BlockSpec(memory_space=pl.ANY)


def _place():
    return lax.axis_index("x"), lax.axis_index("y"), lax.axis_index("c")


def all_gather(shard, name):
    r, c = shard.shape

    def body(x_ref, out_ref, send_sems, recv_sems, local_sem):
        x, y, cc = _place()
        me, sibling = (x, y, cc), (x, y, 1 - cc)
        chips = [(1 - x, y), (x, 1 - y), (1 - x, 1 - y)]

        def slot(px, py, pc):
            return out_ref.at[4 * px + 2 * py + pc]

        def copy(k, block, to, src=None):
            return pltpu.make_async_remote_copy(
                src_ref=slot(*block) if src is None else src, dst_ref=slot(*block),
                send_sem=send_sems.at[k], recv_sem=recv_sems.at[k], device_id=to, device_id_type=MESH_IDS)

        mine = pltpu.make_async_copy(x_ref, slot(*me), local_sem)
        mine.start()
        first = [copy(0, me, sibling, src=x_ref)]
        first += [copy(1 + j, me, (*chip, cc), src=x_ref) for j, chip in enumerate(chips)]
        for cp in first:
            cp.start()
        passed = [copy(4 + j, (*chip, cc), sibling) for j, chip in enumerate(chips)]
        for j, chip in enumerate(chips):
            copy(1 + j, (*chip, cc), me).wait_recv()
            passed[j].start()
        copy(0, sibling, me).wait_recv()
        for j, chip in enumerate(chips):
            copy(4 + j, (*chip, 1 - cc), me).wait_recv()
        for cp in first + passed:
            cp.wait_send()
        mine.wait()

    return pl.pallas_call(
        body, name=name, in_specs=[ANY], out_specs=ANY,
        out_shape=jax.ShapeDtypeStruct((N_DEV, r, c), shard.dtype),
        scratch_shapes=[pltpu.SemaphoreType.DMA((7,)), pltpu.SemaphoreType.DMA((7,)), pltpu.SemaphoreType.DMA],
    )(shard)


def exchange_partials(parts, name):
    _, r, c = parts.shape

    def body(p_ref, out_ref, send_sems, recv_sems, local_sem):
        x, y, cc = _place()
        my_slot = 4 * x + 2 * y + cc

        def peer(m):
            return (x ^ ((m >> 2) & 1), y ^ ((m >> 1) & 1), cc ^ (m & 1))

        def copy(m):
            px, py, pc = peer(m)
            return pltpu.make_async_remote_copy(
                src_ref=p_ref.at[4 * px + 2 * py + pc], dst_ref=out_ref.at[my_slot],
                send_sem=send_sems.at[m - 1], recv_sem=recv_sems.at[m - 1],
                device_id=(px, py, pc), device_id_type=MESH_IDS)

        def arrival(m):
            px, py, pc = peer(m)
            return pltpu.make_async_remote_copy(
                src_ref=p_ref.at[my_slot], dst_ref=out_ref.at[4 * px + 2 * py + pc],
                send_sem=send_sems.at[m - 1], recv_sem=recv_sems.at[m - 1],
                device_id=(px, py, pc), device_id_type=MESH_IDS)

        mine = pltpu.make_async_copy(p_ref.at[my_slot], out_ref.at[my_slot], local_sem)
        mine.start()
        sends = [copy(m) for m in range(1, N_DEV)]
        for cp in sends:
            cp.start()
        for m in range(1, N_DEV):
            arrival(m).wait_recv()
        for cp in sends:
            cp.wait_send()
        mine.wait()

    return pl.pallas_call(
        body, name=name, in_specs=[ANY], out_specs=ANY,
        out_shape=jax.ShapeDtypeStruct(parts.shape, parts.dtype),
        scratch_shapes=[pltpu.SemaphoreType.DMA((7,)), pltpu.SemaphoreType.DMA((7,)), pltpu.SemaphoreType.DMA],
    )(parts)


def sum_slots(parts, name):
    n, r, c = parts.shape
    tr = _pick(r, (256, 128, 64, 32, 16, 8))

    def body(p_ref, o_ref):
        acc = p_ref[0]
        for s in range(1, n):
            acc = acc + p_ref[s]
        o_ref[...] = acc

    return pl.pallas_call(
        body, name=name, grid=(r // tr,),
        in_specs=[pl.BlockSpec((n, tr, c), lambda i: (0, i, 0))],
        out_specs=pl.BlockSpec((tr, c), lambda i: (i, 0)),
        out_shape=jax.ShapeDtypeStruct((r, c), parts.dtype),
        compiler_params=_params(("parallel",)),
    )(parts)


def adamw(w, g, m, v, name):
    shape = w.shape
    cols = shape[-1]
    rows = math.prod(shape[:-1])
    tr = rows if rows <= 512 else _pick(rows, (512, 256, 128, 64, 32, 16, 8))
    two_d = lambda a: a.reshape(rows, cols)

    def body(w_ref, g_ref, m_ref, v_ref, d_ref, nm_ref, nv_ref):
        gv = g_ref[...]
        nm = ADAM_B1 * m_ref[...] + (1.0 - ADAM_B1) * gv
        nv = ADAM_B2 * v_ref[...] + (1.0 - ADAM_B2) * (gv * gv)
        m_hat = nm / (1.0 - ADAM_B1 ** ADAM_STEP)
        v_hat = nv / (1.0 - ADAM_B2 ** ADAM_STEP)
        d_ref[...] = -ADAM_LR * (m_hat / (jnp.sqrt(v_hat) + ADAM_EPS) + ADAM_WD * w_ref[...])
        nm_ref[...] = nm
        nv_ref[...] = nv

    spec = pl.BlockSpec((tr, cols), lambda i: (i, 0))
    out = pl.pallas_call(
        body, name=name, grid=(rows // tr,),
        in_specs=[spec] * 4, out_specs=[spec] * 3,
        out_shape=[jax.ShapeDtypeStruct((rows, cols), F32)] * 3,
        compiler_params=_params(("parallel",)),
    )(two_d(w), two_d(g), two_d(m), two_d(v))
    return tuple(o.reshape(shape) for o in out)


WEIGHT_NAMES = ("sb_w_qkv", "sb_w_o", "gla_w_in", "gla_w_a1", "gla_w_a2", "gla_b_a", "gla_norm_g", "gla_w_o",
                "ffn_w_up", "ffn_conv_w", "ffn_conv_b", "ffn_w_down", "ln_mix_g", "ln_mix_b", "ln_ffn_g", "ln_ffn_b")
SHARD_AXIS = {"sb_w_qkv": 2, "sb_w_o": 1, "gla_w_in": 2, "gla_w_a1": 1, "gla_w_a2": 2, "gla_b_a": 1,
              "gla_norm_g": 1, "gla_w_o": 1, "ffn_w_up": 2, "ffn_conv_w": 2, "ffn_w_down": 1}
MATMUL_WEIGHTS = ("sb_w_qkv", "sb_w_o", "gla_w_in", "gla_w_o", "ffn_w_up", "ffn_w_down")
SMALL_WEIGHTS = ("gla_w_a1", "gla_w_a2", "gla_b_a", "gla_norm_g", "ffn_conv_w")
REPLICATED = ("ffn_conv_b", "ln_mix_g", "ln_mix_b", "ln_ffn_g", "ln_ffn_b")


def _rows_of(n, row_multiple):
    rows = -(-n // PACK_COLS)
    return -(-rows // row_multiple) * row_multiple


def _pack(flat_parts, lead, row_multiple):
    out = []
    for p in flat_parts:
        n = p.shape[-1]
        rows = _rows_of(n, row_multiple)
        pad = rows * PACK_COLS - n
        if pad:
            p = jnp.pad(p, [(0, 0)] * len(lead) + [(0, pad)])
        out.append(p.reshape(*lead, rows, PACK_COLS))
    return jnp.concatenate(out, axis=len(lead))


def _unpack(buf, sizes, row_multiple):
    lead = buf.shape[:-2]
    out, row = [], 0
    for n in sizes:
        r = _rows_of(n, row_multiple)
        piece = lax.slice_in_dim(buf, row, row + r, axis=len(lead)).reshape(*lead, r * PACK_COLS)
        out.append(piece[..., :n])
        row += r
    return out


def _whole_from_shards(stacked, shard_shape, axis):
    a = jnp.moveaxis(stacked.reshape((N_DEV,) + tuple(shard_shape)), 0, axis)
    shape = list(shard_shape)
    shape[axis] *= N_DEV
    return a.reshape(shape)


def _shards_from_whole(whole, axis):
    shape = list(whole.shape)
    a = whole.reshape(shape[:axis] + [N_DEV, shape[axis] // N_DEV] + shape[axis + 1:])
    return jnp.moveaxis(a, axis, 0).reshape(N_DEV, -1)


def kernel(x, sb_w_qkv, sb_w_o, gla_w_in, gla_w_a1, gla_w_a2, gla_b_a, gla_norm_g, gla_w_o, ffn_w_up, ffn_conv_w, ffn_conv_b, ffn_w_down, ln_mix_g, ln_mix_b, ln_ffn_g, ln_ffn_b, loss_target, m_sb_w_qkv, m_sb_w_o, m_gla_w_in, m_gla_w_a1, m_gla_w_a2, m_gla_b_a, m_gla_norm_g, m_gla_w_o, m_ffn_w_up, m_ffn_conv_w, m_ffn_conv_b, m_ffn_w_down, m_ln_mix_g, m_ln_mix_b, m_ln_ffn_g, m_ln_ffn_b, v_sb_w_qkv, v_sb_w_o, v_gla_w_in, v_gla_w_a1, v_gla_w_a2, v_gla_b_a, v_gla_norm_g, v_gla_w_o, v_ffn_w_up, v_ffn_conv_w, v_ffn_conv_b, v_ffn_w_down, v_ln_mix_g, v_ln_mix_b, v_ln_ffn_g, v_ln_ffn_b):
    wts = dict(zip(WEIGHT_NAMES, (sb_w_qkv, sb_w_o, gla_w_in, gla_w_a1, gla_w_a2, gla_b_a, gla_norm_g, gla_w_o,
                                  ffn_w_up, ffn_conv_w, ffn_conv_b, ffn_w_down, ln_mix_g, ln_mix_b, ln_ffn_g, ln_ffn_b)))
    mom1 = dict(zip(WEIGHT_NAMES, (m_sb_w_qkv, m_sb_w_o, m_gla_w_in, m_gla_w_a1, m_gla_w_a2, m_gla_b_a, m_gla_norm_g,
                                   m_gla_w_o, m_ffn_w_up, m_ffn_conv_w, m_ffn_conv_b, m_ffn_w_down, m_ln_mix_g,
                                   m_ln_mix_b, m_ln_ffn_g, m_ln_ffn_b)))
    mom2 = dict(zip(WEIGHT_NAMES, (v_sb_w_qkv, v_sb_w_o, v_gla_w_in, v_gla_w_a1, v_gla_w_a2, v_gla_b_a, v_gla_norm_g,
                                   v_gla_w_o, v_ffn_w_up, v_ffn_conv_w, v_ffn_conv_b, v_ffn_w_down, v_ln_mix_g,
                                   v_ln_mix_b, v_ln_ffn_g, v_ln_ffn_b)))
    t, d = x.shape[1], x.shape[2]
    f = ffn_w_up.shape[2] * N_DEV // 2

    big = all_gather(_pack([wts[n].reshape(-1).astype(MXU_DTYPE) for n in MATMUL_WEIGHTS], (), BF16_ROWS), "gather_matmul_weights")
    small = all_gather(_pack([wts[n].reshape(-1) for n in SMALL_WEIGHTS], (), SUBLANES), "gather_small_weights")
    whole = {}
    for names, buf, mult in ((MATMUL_WEIGHTS, big, BF16_ROWS), (SMALL_WEIGHTS, small, SUBLANES)):
        for n, piece in zip(names, _unpack(buf, [wts[n].size for n in names], mult)):
            whole[n] = _whole_from_shards(piece, wts[n].shape, SHARD_AXIS[n])
    a1 = jnp.pad(whole["gla_w_a1"].astype(MXU_DTYPE), ((0, 0), (0, 0), (0, LANES - GLA_GATE_RANK)))
    w = {
        "sb_qkv": whole["sb_w_qkv"], "sb_o": whole["sb_w_o"],
        "gla_in": jnp.concatenate([whole["gla_w_in"], a1], axis=2), "gla_o": whole["gla_w_o"],
        "up_u": whole["ffn_w_up"][:, :, :f], "up_g": whole["ffn_w_up"][:, :, f:], "down": whole["ffn_w_down"],
        "gla_a2": jnp.pad(whole["gla_w_a2"], ((0, 0), (0, LANES - GLA_GATE_RANK), (0, 0))),
        "gla_ba": whole["gla_b_a"][:, None, :], "gla_ng": whole["gla_norm_g"][:, None, :],
        "cw_u": whole["ffn_conv_w"][:, :, :f], "cw_g": whole["ffn_conv_w"][:, :, f:],
        "cb_u": ffn_conv_b[:, None, :f], "cb_g": ffn_conv_b[:, None, f:],
        "ln_mix_g": ln_mix_g, "ln_mix_b": ln_mix_b, "ln_ffn_g": ln_ffn_g, "ln_ffn_b": ln_ffn_b,
    }

    loss_cols, grad_x, g = local_step(x[0], loss_target[0], w)
    loss = lax.psum(jnp.sum(loss_cols), ("x", "y", "c"))

    full = {
        "sb_w_qkv": g["sb_qkv"], "sb_w_o": g["sb_o"], "gla_w_in": g["gla_in"][:, :, :3 * d],
        "gla_w_a1": g["gla_in"][:, :, 3 * d:3 * d + GLA_GATE_RANK], "gla_w_a2": g["gla_a2"][:, :GLA_GATE_RANK, :],
        "gla_b_a": g["gla_ba"][:, 0], "gla_norm_g": g["gla_ng"][:, 0], "gla_w_o": g["gla_o"],
        "ffn_w_up": jnp.concatenate([g["up_u"], g["up_g"]], axis=2),
        "ffn_conv_w": jnp.concatenate([g["cw_u"], g["cw_g"]], axis=2),
        "ffn_w_down": g["down"],
    }
    sharded = MATMUL_WEIGHTS + SMALL_WEIGHTS
    parts = _pack([_shards_from_whole(full[n], SHARD_AXIS[n]) for n in sharded], (N_DEV,), SUBLANES)
    summed = sum_slots(exchange_partials(parts, "exchange_weight_grads"), "sum_weight_grads")
    grads = {n: piece.reshape(wts[n].shape) for n, piece in zip(sharded, _unpack(summed, [wts[n].size for n in sharded], SUBLANES))}

    rep = {
        "ffn_conv_b": jnp.concatenate([g["cb_u"], g["cb_g"]], axis=2)[:, 0],
        "ln_mix_g": g["ln_mix_g"][:, 0], "ln_mix_b": g["ln_mix_b"][:, 0],
        "ln_ffn_g": g["ln_ffn_g"][:, 0], "ln_ffn_b": g["ln_ffn_b"][:, 0],
    }
    rep_all = all_gather(_pack([rep[n].reshape(-1) for n in REPLICATED], (), SUBLANES), "gather_replicated_grads")
    rep_sum = sum_slots(rep_all, "sum_replicated_grads")
    for n, piece in zip(REPLICATED, _unpack(rep_sum, [wts[n].size for n in REPLICATED], SUBLANES)):
        grads[n] = piece.reshape(wts[n].shape)

    upd = {n: adamw(wts[n], grads[n], mom1[n], mom2[n], f"adamw_{n}") for n in WEIGHT_NAMES}
    return (loss, grad_x[None], *[grads[n] for n in WEIGHT_NAMES], *[upd[n][0] for n in WEIGHT_NAMES],
            *[upd[n][1] for n in WEIGHT_NAMES], *[upd[n][2] for n in WEIGHT_NAMES])
```
